```python
import jax, jax.numpy as jnp
from jax import lax
import numpy as np

D_MODEL = 1024
BATCH = 2
SEQ = 8192
DEPTH = 2

HEAD_DIM = 64
EPS = 1e-6
N_BRANCH = 3
CONV_WIDTH = D_MODEL
CONV_KERNEL = 31
RET_HEADS = 4
RET_QK_DIM = D_MODEL // RET_HEADS
RET_V_DIM = 2 * RET_QK_DIM
RET_CHUNK = 128
RET_THETA = 10000.0
ATTN_HEADS = D_MODEL // HEAD_DIM
ATTN_KV_HEADS = 4
ATTN_WINDOW = 128
ATTN_BLOCK = 128
ROPE_THETA = 500000.0
ROPE_DIM = HEAD_DIM // 4

IN_SPLITS = (
    2 * CONV_WIDTH,
    CONV_WIDTH,
    RET_HEADS * RET_QK_DIM,
    RET_HEADS * RET_QK_DIM,
    RET_HEADS * RET_V_DIM,
    RET_HEADS * RET_V_DIM,
    ATTN_HEADS * HEAD_DIM,
    ATTN_KV_HEADS * HEAD_DIM,
    ATTN_KV_HEADS * HEAD_DIM,
    ATTN_HEADS * HEAD_DIM,
    N_BRANCH * D_MODEL,
)
IN_WIDTH = sum(IN_SPLITS)

kernel_name = 'hybrid_conv_retention_swa_encoder'


def rms_norm(x, g):
    xf = x.astype(jnp.float32)
    y = xf * lax.rsqrt(jnp.mean(xf * xf, axis=-1, keepdims=True) + EPS) * g.astype(jnp.float32)
    return y.astype(x.dtype)


def rotary(x, positions, theta, rot_dim):
    half = rot_dim // 2
    inv_freq = theta ** (-jnp.arange(half, dtype=jnp.float32) / half)
    ang = positions.astype(jnp.float32)[:, None] * inv_freq[None, :]
    cos = jnp.cos(ang)[:, None, :]
    sin = jnp.sin(ang)[:, None, :]
    xf = x.astype(jnp.float32)
    x1 = xf[..., :half]
    x2 = xf[..., half:rot_dim]
    out = jnp.concatenate([x1 * cos - x2 * sin, x2 * cos + x1 * sin, xf[..., rot_dim:]], axis=-1)
    return out.astype(x.dtype)


def conv_module(u_glu, gate, dw, db, ln_g, ln_b):
    a, b = jnp.split(u_glu, 2, axis=-1)
    v = a * jax.nn.sigmoid(b)
    y = lax.conv_general_dilated(
        v, dw[:, None, :].astype(v.dtype), window_strides=(1,),
        padding=[(CONV_KERNEL // 2, CONV_KERNEL // 2)],
        dimension_numbers=('NWC', 'WIO', 'NWC'),
        feature_group_count=CONV_WIDTH) + db.astype(v.dtype)
    yf = y.astype(jnp.float32)
    mu = jnp.mean(yf, axis=-1, keepdims=True)
    var = jnp.mean(jnp.square(yf - mu), axis=-1, keepdims=True)
    yf = (yf - mu) * lax.rsqrt(var + EPS) * ln_g.astype(jnp.float32) + ln_b.astype(jnp.float32)
    y = jax.nn.silu(yf).astype(v.dtype)
    return y * jax.nn.silu(gate)


def retention_one_direction(q, k, v, log_gamma, inclusive):
    B, H, S, dk = q.shape
    dv = v.shape[-1]
    C = RET_CHUNK
    n = S // C
    qc = q.reshape(B, H, n, C, dk)
    kc = k.reshape(B, H, n, C, dk)
    vc = v.reshape(B, H, n, C, dv)
    idx = jnp.arange(C, dtype=jnp.float32)
    diff = idx[:, None] - idx[None, :]
    mask = (diff >= 0) if inclusive else (diff > 0)
    lg = log_gamma[:, None, None]
    decay_inner = jnp.where(mask[None], jnp.exp(lg * jnp.where(mask, diff, 0.0)[None]), 0.0)
    scores = jnp.einsum('bhnid,bhnjd->bhnij', qc, kc) * decay_inner[None, :, None]
    inner = jnp.einsum('bhnij,bhnjv->bhniv', scores, vc)
    q_decay = jnp.exp(log_gamma[:, None] * (idx[None, :] + 1.0))[None, :, :, None]
    k_decay = jnp.exp(log_gamma[:, None] * (C - 1.0 - idx[None, :]))[None, :, :, None]
    chunk_decay = jnp.exp(log_gamma * C)[None, :, None, None]

    def step(state, xs):
        q_t, k_t, v_t = xs
        cross = jnp.einsum('bhcd,bhdv->bhcv', q_t * q_decay, state)
        state = chunk_decay * state + jnp.einsum('bhcd,bhcv->bhdv', k_t * k_decay, v_t)
        return state, cross

    state0 = jnp.zeros((B, H, dk, dv), jnp.float32)
    xs = (jnp.moveaxis(qc, 2, 0), jnp.moveaxis(kc, 2, 0), jnp.moveaxis(vc, 2, 0))
    _, cross = lax.scan(step, state0, xs)
    out = inner + jnp.moveaxis(cross, 0, 2)
    return out.reshape(B, H, S, dv)


def retention_branch(q, k, v, gate, ret_decay, positions):
    B, S, _ = q.shape
    q = rotary(q.reshape(B, S, RET_HEADS, RET_QK_DIM), positions, RET_THETA, RET_QK_DIM)
    k = rotary(k.reshape(B, S, RET_HEADS, RET_QK_DIM), positions, RET_THETA, RET_QK_DIM)
    qf = jnp.transpose(q, (0, 2, 1, 3)).astype(jnp.float32)
    kf = jnp.transpose(k, (0, 2, 1, 3)).astype(jnp.float32) * (RET_QK_DIM ** -0.5)
    vf = jnp.transpose(v.reshape(B, S, RET_HEADS, RET_V_DIM), (0, 2, 1, 3)).astype(jnp.float32)
    log_gamma = -jnp.exp(ret_decay.astype(jnp.float32))
    fwd = retention_one_direction(qf, kf, vf, log_gamma[0], True)
    bwd = retention_one_direction(qf[:, :, ::-1], kf[:, :, ::-1], vf[:, :, ::-1], log_gamma[1], False)[:, :, ::-1]
    o = fwd + bwd
    mu = jnp.mean(o, axis=-1, keepdims=True)
    var = jnp.mean(jnp.square(o - mu), axis=-1, keepdims=True)
    o = (o - mu) * lax.rsqrt(var + EPS)
    o = jnp.transpose(o, (0, 2, 1, 3)).reshape(B, S, RET_HEADS * RET_V_DIM).astype(gate.dtype)
    return o * jax.nn.silu(gate)


def window_attention_branch(q, k, v, gate, q_norm_g, k_norm_g, sink, positions):
    B, S, _ = q.shape
    G = ATTN_HEADS // ATTN_KV_HEADS
    T = ATTN_BLOCK
    nb = S // T
    q = rms_norm(q.reshape(B, S, ATTN_HEADS, HEAD_DIM), q_norm_g)
    k = rms_norm(k.reshape(B, S, ATTN_KV_HEADS, HEAD_DIM), k_norm_g)
    q = rotary(q, positions, ROPE_THETA, ROPE_DIM)
    k = rotary(k, positions, ROPE_THETA, ROPE_DIM)
    v = v.reshape(B, S, ATTN_KV_HEADS, HEAD_DIM)
    qb = q.reshape(B, nb, T, ATTN_KV_HEADS, G, HEAD_DIM).astype(jnp.float32)
    pad = ((0, 0), (T, T), (0, 0), (0, 0))
    kb = jnp.pad(k, pad).reshape(B, nb + 2, T, ATTN_KV_HEADS, HEAD_DIM)
    vb = jnp.pad(v, pad).reshape(B, nb + 2, T, ATTN_KV_HEADS, HEAD_DIM)
    kwin = jnp.concatenate([kb[:, :-2], kb[:, 1:-1], kb[:, 2:]], axis=2).astype(jnp.float32)
    vwin = jnp.concatenate([vb[:, :-2], vb[:, 1:-1], vb[:, 2:]], axis=2).astype(jnp.float32)
    s = jnp.einsum('bnqhgd,bnjhd->bhgnqj', qb, kwin) * (HEAD_DIM ** -0.5)
    blk = jnp.arange(nb)[:, None, None]
    qpos = blk * T + jnp.arange(T)[None, :, None]
    kpos = (blk - 1) * T + jnp.arange(3 * T)[None, None, :]
    valid = (jnp.abs(kpos - qpos) <= ATTN_WINDOW) & (kpos >= 0) & (kpos < S)
    s = jnp.where(valid, s, -1e30)
    sink_l = sink.astype(jnp.float32).reshape(ATTN_KV_HEADS, G, 1, 1, 1)
    m = jnp.maximum(jnp.max(s, axis=-1, keepdims=True), sink_l)
    p = jnp.exp(s - m)
    p = p / (jnp.sum(p, axis=-1, keepdims=True) + jnp.exp(sink_l - m))
    o = jnp.einsum('bhgnqj,bnjhd->bnqhgd', p, vwin)
    o = o.reshape(B, S, ATTN_HEADS * HEAD_DIM).astype(gate.dtype)
    return o * jax.nn.silu(gate)


def hybrid_layer(x, positions, norm_g, w_in, b_gate, conv_dw, conv_b, conv_ln_g, conv_ln_b,
                 ret_decay, q_norm_g, k_norm_g, attn_sink, w_conv_out, w_ret_out, w_attn_out, w_out):
    B, S, D = x.shape
    h = rms_norm(x, norm_g)
    z = h @ w_in
    offsets = np.cumsum(IN_SPLITS)[:-1].tolist()
    (c_glu, c_gate, r_q, r_k, r_v, r_gate, a_q, a_k, a_v, a_gate, g_logit) = jnp.split(z, offsets, axis=-1)
    y_conv = conv_module(c_glu, c_gate, conv_dw, conv_b, conv_ln_g, conv_ln_b) @ w_conv_out
    y_ret = retention_branch(r_q, r_k, r_v, r_gate, ret_decay, positions) @ w_ret_out
    y_attn = window_attention_branch(a_q, a_k, a_v, a_gate, q_norm_g, k_norm_g, attn_sink, positions) @ w_attn_out
    g = jax.nn.sigmoid(g_logit + b_gate).reshape(B, S, N_BRANCH, D)
    merged = g[:, :, 0] * y_conv + g[:, :, 1] * y_ret + g[:, :, 2] * y_attn
    return x + merged @ w_out


def setup_inputs(seed: int = 0) -> dict:
    key = jax.random.key(seed)
    ks = jax.random.split(key, 18)
    nrm = jax.random.normal
    f32 = jnp.float32
    ret_base = jnp.log(-jnp.log1p(-(2.0 ** (-5.0 - jnp.arange(RET_HEADS, dtype=f32)))))
    return {
        'x': nrm(ks[0], (BATCH, SEQ, D_MODEL), f32),
        'norm_g': 1.0 + 0.02 * nrm(ks[1], (DEPTH, D_MODEL), f32),
        'w_in': nrm(ks[2], (DEPTH, D_MODEL, IN_WIDTH), f32) * D_MODEL ** -0.5,
        'b_gate': 0.1 * nrm(ks[3], (DEPTH, N_BRANCH * D_MODEL), f32),
        'conv_dw': nrm(ks[4], (DEPTH, CONV_KERNEL, CONV_WIDTH), f32) * CONV_KERNEL ** -0.5,
        'conv_b': 0.02 * nrm(ks[5], (DEPTH, CONV_WIDTH), f32),
        'conv_ln_g': 1.0 + 0.02 * nrm(ks[6], (DEPTH, CONV_WIDTH), f32),
        'conv_ln_b': 0.02 * nrm(ks[7], (DEPTH, CONV_WIDTH), f32),
        'ret_decay': ret_base + 0.05 * nrm(ks[8], (DEPTH, 2, RET_HEADS), f32),
        'q_norm_g': 1.0 + 0.02 * nrm(ks[9], (DEPTH, HEAD_DIM), f32),
        'k_norm_g': 1.0 + 0.02 * nrm(ks[10], (DEPTH, HEAD_DIM), f32),
        'attn_sink': 0.5 * nrm(ks[11], (DEPTH, ATTN_HEADS), f32),
        'w_conv_out': nrm(ks[12], (DEPTH, CONV_WIDTH, D_MODEL), f32) * CONV_WIDTH ** -0.5,
        'w_ret_out': nrm(ks[13], (DEPTH, RET_HEADS * RET_V_DIM, D_MODEL), f32) * (RET_HEADS * RET_V_DIM) ** -0.5,
        'w_attn_out': nrm(ks[14], (DEPTH, ATTN_HEADS * HEAD_DIM, D_MODEL), f32) * (ATTN_HEADS * HEAD_DIM) ** -0.5,
        'w_out': nrm(ks[15], (DEPTH, D_MODEL, D_MODEL), f32) * D_MODEL ** -0.5,
    }


def reference(x, norm_g, w_in, b_gate, conv_dw, conv_b, conv_ln_g, conv_ln_b, ret_decay,
              q_norm_g, k_norm_g, attn_sink, w_conv_out, w_ret_out, w_attn_out, w_out):
    positions = jnp.arange(x.shape[1], dtype=jnp.int32)
    for l in range(DEPTH):
        x = hybrid_layer(x, positions, norm_g[l], w_in[l], b_gate[l], conv_dw[l], conv_b[l],
                         conv_ln_g[l], conv_ln_b[l], ret_decay[l], q_norm_g[l], k_norm_g[l],
                         attn_sink[l], w_conv_out[l], w_ret_out[l], w_attn_out[l], w_out[l])
    return x
```

```python
import functools

import jax
import jax.numpy as jnp
from jax import lax
from jax.experimental import pallas as pl
from jax.experimental.pallas import tpu as pltpu

F32 = jnp.float32
BF16 = jnp.bfloat16

D_MODEL = 1024
HEAD_DIM = 64
EPS = 1e-6
CONV_KERNEL = 31
CONV_HALF = CONV_KERNEL // 2
RET_HEADS = 4
RET_QK_DIM = 256
RET_V_DIM = 512
RET_THETA = 10000.0
ATTN_HEADS = 16
ATTN_KV_HEADS = 4
ATTN_GROUP = ATTN_HEADS // ATTN_KV_HEADS
ATTN_BLOCK = 128
ROPE_THETA = 500000.0
ROPE_DIM = HEAD_DIM // 4

LANES = 128
VMEM_LIMIT_BYTES = 56 * 1024 * 1024

Z_CONV_A, Z_CONV_B, Z_CONV_GATE = 0, 1, 2
Z_RET_Q, Z_RET_K, Z_RET_V, Z_RET_GATE = 3, 4, 5, 7
Z_ATT_Q, Z_ATT_K, Z_ATT_V, Z_ATT_GATE = 9, 10, 11, 12
Z_MERGE = 13
Z_WIDTH = 16 * D_MODEL

PROJ_TM, PROJ_TN, PROJ_ROWS = 2048, 512, 512
NORM_ROWS = 256
CONV_TILE, CONV_ROWS, CONV_HALO, CONV_TOKENS = 512, 32, 16, 16
CBLK = D_MODEL // LANES
RET_CHUNK = 256
PREP_TILE = 512
OUT_TM = 256


def _sigmoid(x):
    return 1.0 / (1.0 + jnp.exp(-x))


def _silu(x):
    return x * _sigmoid(x)


def _in_proj_kernel(x_ref, g_ref, w_ref, z_ref, h_ref):
    @pl.when(pl.program_id(1) == 0)
    def _():
        def norm_rows(r, carry):
            rows = pl.ds(pl.multiple_of(r * NORM_ROWS, NORM_ROWS), NORM_ROWS)
            xv = x_ref[rows, :]
            ms = jnp.mean(xv * xv, axis=-1, keepdims=True)
            h_ref[rows, :] = (xv * lax.rsqrt(ms + EPS) * g_ref[...]).astype(BF16)
            return carry

        lax.fori_loop(0, PROJ_TM // NORM_ROWS, norm_rows, 0)

    def mm_rows(r, carry):
        rows = pl.ds(pl.multiple_of(r * PROJ_ROWS, PROJ_ROWS), PROJ_ROWS)
        z_ref[rows, :] = jnp.dot(h_ref[rows, :], w_ref[...], preferred_element_type=F32).astype(BF16)
        return carry

    lax.fori_loop(0, PROJ_TM // PROJ_ROWS, mm_rows, 0)


def _in_proj(x2, norm_g, w_ext):
    n = x2.shape[0]
    return pl.pallas_call(
        _in_proj_kernel,
        out_shape=jax.ShapeDtypeStruct((n, Z_WIDTH), BF16),
        grid=(n // PROJ_TM, Z_WIDTH // PROJ_TN),
        in_specs=[
            pl.BlockSpec((PROJ_TM, D_MODEL), lambda i, j: (i, 0)),
            pl.BlockSpec((1, D_MODEL), lambda i, j: (0, 0)),
            pl.BlockSpec((D_MODEL, PROJ_TN), lambda i, j: (0, j)),
        ],
        out_specs=pl.BlockSpec((PROJ_TM, PROJ_TN), lambda i, j: (i, j)),
        scratch_shapes=[pltpu.VMEM((PROJ_TM, D_MODEL), BF16)],
        compiler_params=pltpu.CompilerParams(
            dimension_semantics=("arbitrary", "arbitrary"), vmem_limit_bytes=VMEM_LIMIT_BYTES),
        name="in_proj",
    )(x2, norm_g.reshape(1, D_MODEL), w_ext)


def _conv_kernel(seq_tiles, a_ref, b_ref, gate_ref, ap_ref, bp_ref, an_ref, bn_ref,
                 w_ref, cb_ref, lg_ref, lb_ref, out_ref, v_ref, y_ref):
    t = pl.program_id(0) % seq_tiles

    def glu(a, b):
        return a.astype(F32) * _sigmoid(b.astype(F32))

    def scatter(val, tok0):
        for c in range(CBLK):
            v_ref[pl.ds(tok0 * CBLK + c, val.shape[0], stride=CBLK), :] = val[:, c * LANES:(c + 1) * LANES]

    scatter(jnp.where(t == 0, 0.0, glu(ap_ref[...], bp_ref[...])), 0)
    scatter(jnp.where(t == seq_tiles - 1, 0.0, glu(an_ref[...], bn_ref[...])), CONV_HALO + CONV_TILE)

    def fill(r, carry):
        r0 = pl.multiple_of(r * CONV_ROWS, CONV_ROWS)
        rows = pl.ds(r0, CONV_ROWS)
        scatter(glu(a_ref[rows, :], b_ref[rows, :]), r0 + CONV_HALO)
        return carry

    lax.fori_loop(0, CONV_TILE // CONV_ROWS, fill, 0)

    bias = cb_ref[...]

    def conv_tokens(r, carry):
        t0 = r * CONV_TOKENS
        acc = [bias] * CONV_TOKENS
        for k in range(CONV_KERNEL):
            wk = w_ref[k * CBLK:(k + 1) * CBLK, :]
            for i in range(CONV_TOKENS):
                src = pl.multiple_of((t0 + (i + k - CONV_HALF + CONV_HALO)) * CBLK, CBLK)
                acc[i] = acc[i] + v_ref[pl.ds(src, CBLK), :] * wk
        for i in range(CONV_TOKENS):
            y_ref[pl.ds(pl.multiple_of((t0 + i) * CBLK, CBLK), CBLK), :] = acc[i]
        return carry

    lax.fori_loop(0, CONV_TILE // CONV_TOKENS, conv_tokens, 0)

    def norm_rows(r, carry):
        r0 = pl.multiple_of(r * CONV_ROWS, CONV_ROWS)
        y = jnp.concatenate(
            [y_ref[pl.ds(r0 * CBLK + c, CONV_ROWS, stride=CBLK), :] for c in range(CBLK)], axis=1)
        mu = jnp.mean(y, axis=-1, keepdims=True)
        d = y - mu
        var = jnp.mean(d * d, axis=-1, keepdims=True)
        yn = d * lax.rsqrt(var + EPS) * lg_ref[...] + lb_ref[...]
        g = gate_ref[pl.ds(r0, CONV_ROWS), :].astype(F32)
        out_ref[pl.ds(r0, CONV_ROWS), :] = (_silu(yn) * _silu(g)).astype(BF16)
        return carry

    lax.fori_loop(0, CONV_TILE // CONV_ROWS, norm_rows, 0)


def _conv_branch(z, seq, conv_dw, conv_b, ln_g, ln_b):
    n = z.shape[0]
    seq_tiles = seq // CONV_TILE
    halo_per_tile = CONV_TILE // CONV_HALO
    last_halo = n // CONV_HALO - 1
    w_tiles = conv_dw.reshape(CONV_KERNEL * CBLK, LANES)

    def main(col):
        return pl.BlockSpec((CONV_TILE, D_MODEL), lambda i: (i, col))

    def prev(col):
        return pl.BlockSpec((CONV_HALO, D_MODEL), lambda i: (jnp.maximum(i * halo_per_tile - 1, 0), col))

    def nxt(col):
        return pl.BlockSpec((CONV_HALO, D_MODEL), lambda i: (jnp.minimum((i + 1) * halo_per_tile, last_halo), col))

    vec = pl.BlockSpec((1, D_MODEL), lambda i: (0, 0))
    return pl.pallas_call(
        functools.partial(_conv_kernel, seq_tiles),
        out_shape=jax.ShapeDtypeStruct((n, D_MODEL), BF16),
        grid=(n // CONV_TILE,),
        in_specs=[main(Z_CONV_A), main(Z_CONV_B), main(Z_CONV_GATE),
                  prev(Z_CONV_A), prev(Z_CONV_B), nxt(Z_CONV_A), nxt(Z_CONV_B),
                  pl.BlockSpec((CONV_KERNEL * CBLK, LANES), lambda i: (0, 0)),
                  pl.BlockSpec((CBLK, LANES), lambda i: (0, 0)), vec, vec],
        out_specs=pl.BlockSpec((CONV_TILE, D_MODEL), lambda i: (i, 0)),
        scratch_shapes=[pltpu.VMEM(((CONV_TILE + 2 * CONV_HALO) * CBLK, LANES), F32),
                        pltpu.VMEM((CONV_TILE * CBLK, LANES), F32)],
        compiler_params=pltpu.CompilerParams(
            dimension_semantics=("arbitrary",), vmem_limit_bytes=VMEM_LIMIT_BYTES),
        name="conv_branch",
    )(z, z, z, z, z, z, z, w_tiles, conv_b.reshape(CBLK, LANES), ln_g.reshape(1, D_MODEL),
      ln_b.reshape(1, D_MODEL))


def _ret_kernel(n_chunks, rd_ref, q_ref, k_ref, v_ref, gate_ref, cos_ref, sin_ref, out_ref,
                oacc_ref, state_ref, dmat_ref, qdec_ref, kdec_ref, cdec_ref):
    L = RET_CHUNK
    head = pl.program_id(0) % RET_HEADS
    p = pl.program_id(1)
    c = pl.program_id(2)
    cc = c + p * (n_chunks - 1 - 2 * c)

    @pl.when(c == 0)
    def _():
        log_gamma = -jnp.exp(jnp.full((L, LANES), rd_ref[p, head], F32))
        idx = lax.broadcasted_iota(jnp.int32, (L, LANES), 0).astype(F32)
        q_exp = jnp.where(p == 0, idx + 1.0, L - idx)
        k_exp = jnp.where(p == 0, L - 1.0 - idx, idx)
        qdec_ref[...] = jnp.exp(log_gamma * q_exp)
        kdec_ref[...] = jnp.exp(log_gamma * k_exp)
        cdec_ref[...] = jnp.exp(log_gamma[:RET_QK_DIM] * float(L))
        state_ref[...] = jnp.zeros_like(state_ref)

    @pl.when((c == 0) & (p == 0))
    def _():
        lg_f = -jnp.exp(jnp.full((L, L), rd_ref[0, head], F32))
        lg_b = -jnp.exp(jnp.full((L, L), rd_ref[1, head], F32))
        i = lax.broadcasted_iota(jnp.int32, (L, L), 0)
        j = lax.broadcasted_iota(jnp.int32, (L, L), 1)
        d = (i - j).astype(F32)
        dmat_ref[...] = jnp.where(i >= j, jnp.exp(lg_f * jnp.maximum(d, 0.0)),
                                  jnp.exp(lg_b * jnp.maximum(-d, 0.0)))

    cosv = cos_ref[...]
    sinv = sin_ref[...]

    def rotate(ref, scale):
        x1 = ref[:, :LANES].astype(F32) * scale
        x2 = ref[:, LANES:].astype(F32) * scale
        return x1 * cosv - x2 * sinv, x2 * cosv + x1 * sinv

    q1, q2 = rotate(q_ref, 1.0)
    k1, k2 = rotate(k_ref, RET_QK_DIM ** -0.5)
    qd = qdec_ref[...]
    kd = kdec_ref[...]
    q_dec = jnp.concatenate([q1 * qd, q2 * qd], axis=1).astype(BF16)
    k_dec = jnp.concatenate([k1 * kd, k2 * kd], axis=1).astype(BF16)
    vb = v_ref[...]
    state = state_ref[...]
    cross = jnp.dot(q_dec, state.astype(BF16), preferred_element_type=F32)
    kv = lax.dot_general(k_dec, vb, (((0,), (0,)), ((), ())), preferred_element_type=F32)
    cd = cdec_ref[...]
    state_ref[...] = jnp.concatenate([cd] * (RET_V_DIM // LANES), axis=1) * state + kv
    rows = pl.ds(pl.multiple_of(cc * L, L), L)

    @pl.when(p == 0)
    def _():
        qb = jnp.concatenate([q1, q2], axis=1).astype(BF16)
        kb = jnp.concatenate([k1, k2], axis=1).astype(BF16)
        s = lax.dot_general(qb, kb, (((1,), (1,)), ((), ())), preferred_element_type=F32)
        a = (s * dmat_ref[...]).astype(BF16)
        oacc_ref[rows, :] = jnp.dot(a, vb, preferred_element_type=F32) + cross

    @pl.when(p == 1)
    def _():
        o = oacc_ref[rows, :] + cross
        mu = jnp.mean(o, axis=-1, keepdims=True)
        d = o - mu
        var = jnp.mean(d * d, axis=-1, keepdims=True)
        g = gate_ref[...].astype(F32)
        out_ref[...] = (d * lax.rsqrt(var + EPS) * _silu(g)).astype(BF16)


def _ret_branch(z, batch, seq, ret_decay, cos_r, sin_r):
    n = z.shape[0]
    L = RET_CHUNK
    n_chunks = seq // L
    qk_blk = D_MODEL // RET_QK_DIM
    v_blk = D_MODEL // RET_V_DIM

    def chunk(p, c):
        return c + p * (n_chunks - 1 - 2 * c)

    def row(bh, p, c):
        return (bh // RET_HEADS) * n_chunks + chunk(p, c)

    def last_row(bh, p, c):
        return (bh // RET_HEADS) * n_chunks + (n_chunks - 1 - p * c)

    in_specs = [
        pl.BlockSpec(memory_space=pltpu.SMEM),
        pl.BlockSpec((L, RET_QK_DIM), lambda bh, p, c: (row(bh, p, c), Z_RET_Q * qk_blk + bh % RET_HEADS)),
        pl.BlockSpec((L, RET_QK_DIM), lambda bh, p, c: (row(bh, p, c), Z_RET_K * qk_blk + bh % RET_HEADS)),
        pl.BlockSpec((L, RET_V_DIM), lambda bh, p, c: (row(bh, p, c), Z_RET_V * v_blk + bh % RET_HEADS)),
        pl.BlockSpec((L, RET_V_DIM), lambda bh, p, c: (last_row(bh, p, c), Z_RET_GATE * v_blk + bh % RET_HEADS)),
        pl.BlockSpec((L, LANES), lambda bh, p, c: (chunk(p, c), 0)),
        pl.BlockSpec((L, LANES), lambda bh, p, c: (chunk(p, c), 0)),
    ]
    return pl.pallas_call(
        functools.partial(_ret_kernel, n_chunks),
        out_shape=jax.ShapeDtypeStruct((n, RET_HEADS * RET_V_DIM), BF16),
        grid=(batch * RET_HEADS, 2, n_chunks),
        in_specs=in_specs,
        out_specs=pl.BlockSpec((L, RET_V_DIM), lambda bh, p, c: (last_row(bh, p, c), bh % RET_HEADS)),
        scratch_shapes=[
            pltpu.VMEM((seq, RET_V_DIM), F32),
            pltpu.VMEM((RET_QK_DIM, RET_V_DIM), F32),
            pltpu.VMEM((L, L), F32),
            pltpu.VMEM((L, LANES), F32),
            pltpu.VMEM((L, LANES), F32),
            pltpu.VMEM((RET_QK_DIM, LANES), F32),
        ],
        compiler_params=pltpu.CompilerParams(
            dimension_semantics=("arbitrary", "arbitrary", "arbitrary"),
            vmem_limit_bytes=VMEM_LIMIT_BYTES),
        name="retention",
    )(ret_decay, z, z, z, z, cos_r, sin_r)


def _attn_prep_kernel(q_ref, k_ref, c_ref, s1_ref, s2_ref, gq_ref, gk_ref, bd_ref, qp_ref, kp_ref):
    cosv = c_ref[...]
    sin_lo = s1_ref[...]
    sin_hi = s2_ref[...]
    bd = bd_ref[...]

    def prep(src, dst, g_ref, scale):
        for c in range(D_MODEL // LANES):
            lanes = slice(c * LANES, (c + 1) * LANES)
            x = src[:, lanes].astype(F32)
            ms = jnp.dot((x * x).astype(BF16), bd, preferred_element_type=F32)
            xn = x * lax.rsqrt(ms + EPS) * g_ref[...]
            xr = (xn * cosv + pltpu.roll(xn, LANES - ROPE_DIM // 2, 1) * sin_lo
                  + pltpu.roll(xn, ROPE_DIM // 2, 1) * sin_hi)
            dst[:, lanes] = (xr * scale).astype(BF16)

    prep(q_ref, qp_ref, gq_ref, HEAD_DIM ** -0.5)
    prep(k_ref, kp_ref, gk_ref, 1.0)


def _attn_prep(z, seq, rope_c, rope_s1, rope_s2, q_norm_g, k_norm_g):
    n = z.shape[0]
    seq_tiles = seq // PREP_TILE
    per_lane_block = LANES // HEAD_DIM
    gq = jnp.tile(q_norm_g, per_lane_block).reshape(1, LANES)
    gk = jnp.tile(k_norm_g, per_lane_block).reshape(1, LANES)
    lane_head = jnp.arange(LANES) // HEAD_DIM
    bd = jnp.where(lane_head[:, None] == lane_head[None, :], 1.0 / HEAD_DIM, 0.0).astype(BF16)
    tab = pl.BlockSpec((PREP_TILE, LANES), lambda i: (i % seq_tiles, 0))
    vec = pl.BlockSpec((1, LANES), lambda i: (0, 0))
    out = jax.ShapeDtypeStruct((n, D_MODEL), BF16)
    return pl.pallas_call(
        _attn_prep_kernel,
        out_shape=(out, out),
        grid=(n // PREP_TILE,),
        in_specs=[pl.BlockSpec((PREP_TILE, D_MODEL), lambda i: (i, Z_ATT_Q)),
                  pl.BlockSpec((PREP_TILE, D_MODEL), lambda i: (i, Z_ATT_K)),
                  tab, tab, tab, vec, vec,
                  pl.BlockSpec((LANES, LANES), lambda i: (0, 0))],
        out_specs=(pl.BlockSpec((PREP_TILE, D_MODEL), lambda i: (i, 0)),
                   pl.BlockSpec((PREP_TILE, D_MODEL), lambda i: (i, 0))),
        compiler_params=pltpu.CompilerParams(
            dimension_semantics=("arbitrary",), vmem_limit_bytes=VMEM_LIMIT_BYTES),
        name="attn_prep",
    )(z, z, rope_c, rope_s1, rope_s2, gq, gk, bd)


def _attn_kernel(seq_blocks, sink_ref, q_ref, kl_ref, km_ref, kr_ref, vl_ref, vm_ref, vr_ref,
                 gate_ref, out_ref):
    T = ATTN_BLOCK
    GW = ATTN_GROUP * HEAD_DIM
    blk = pl.program_id(0) % seq_blocks
    row = lax.broadcasted_iota(jnp.int32, (T, T), 0)
    col = lax.broadcasted_iota(jnp.int32, (T, T), 1)
    mask_l = (col >= row) & (blk > 0)
    mask_r = (col <= row) & (blk < seq_blocks - 1)
    lane = lax.broadcasted_iota(jnp.int32, (T, GW), 1)
    head_masks = [jnp.where(lane // HEAD_DIM == h, 1.0, 0.0).astype(BF16) for h in range(ATTN_GROUP)]
    lane_o = lax.broadcasted_iota(jnp.int32, (T, LANES), 1)

    def stack_heads(x):
        return jnp.concatenate([x * m for m in head_masks], axis=0)

    for g in range(ATTN_KV_HEADS):
        lanes = slice(g * GW, (g + 1) * GW)
        qg = q_ref[:, lanes]
        scores = [lax.dot_general(qg, stack_heads(k_ref[:, lanes]), (((1,), (1,)), ((), ())),
                                  preferred_element_type=F32)
                  for k_ref in (kl_ref, km_ref, kr_ref)]
        probs = [[], [], []]
        inv_l = []
        for h in range(ATTN_GROUP):
            hs = slice(h * T, (h + 1) * T)
            sl = jnp.where(mask_l, scores[0][:, hs], -1e30)
            sm = scores[1][:, hs]
            sr = jnp.where(mask_r, scores[2][:, hs], -1e30)
            sink = sink_ref[g * ATTN_GROUP + h]
            m = jnp.maximum(jnp.max(jnp.maximum(jnp.maximum(sl, sm), sr), axis=-1, keepdims=True), sink)
            el = jnp.exp(sl - m)
            em = jnp.exp(sm - m)
            er = jnp.exp(sr - m)
            denom = jnp.sum(el + em + er, axis=-1, keepdims=True) + jnp.exp(sink - m)
            inv_l.append(1.0 / denom)
            probs[0].append(el.astype(BF16))
            probs[1].append(em.astype(BF16))
            probs[2].append(er.astype(BF16))
        o = jnp.zeros((T, GW), F32)
        for w, v_ref in enumerate((vl_ref, vm_ref, vr_ref)):
            o = o + jnp.dot(jnp.concatenate(probs[w], axis=1), stack_heads(v_ref[:, lanes]),
                            preferred_element_type=F32)
        norm = []
        for hb in range(GW // LANES):
            lo = jnp.broadcast_to(inv_l[2 * hb], (T, LANES))
            hi = jnp.broadcast_to(inv_l[2 * hb + 1], (T, LANES))
            norm.append(jnp.where(lane_o < HEAD_DIM, lo, hi))
        gate = gate_ref[:, lanes].astype(F32)
        out_ref[:, lanes] = (o * jnp.concatenate(norm, axis=1) * _silu(gate)).astype(BF16)


def _attn_branch(z, qp, kp, seq, attn_sink):
    n = z.shape[0]
    T = ATTN_BLOCK
    seq_blocks = seq // T
    last = n // T - 1

    def here(col):
        return lambda i: (i, col)

    def left(col):
        return lambda i: (jnp.maximum(i - 1, 0), col)

    def right(col):
        return lambda i: (jnp.minimum(i + 1, last), col)

    blk = (T, D_MODEL)
    return pl.pallas_call(
        functools.partial(_attn_kernel, seq_blocks),
        out_shape=jax.ShapeDtypeStruct((n, D_MODEL), BF16),
        grid=(n // T,),
        in_specs=[pl.BlockSpec(memory_space=pltpu.SMEM),
                  pl.BlockSpec(blk, here(0)),
                  pl.BlockSpec(blk, left(0)), pl.BlockSpec(blk, here(0)), pl.BlockSpec(blk, right(0)),
                  pl.BlockSpec(blk, left(Z_ATT_V)), pl.BlockSpec(blk, here(Z_ATT_V)),
                  pl.BlockSpec(blk, right(Z_ATT_V)),
                  pl.BlockSpec(blk, here(Z_ATT_GATE))],
        out_specs=pl.BlockSpec(blk, here(0)),
        compiler_params=pltpu.CompilerParams(
            dimension_semantics=("arbitrary",), vmem_limit_bytes=VMEM_LIMIT_BYTES),
        name="window_attn",
    )(attn_sink, qp, kp, kp, kp, z, z, z, z)


def _out_kernel(x_ref, uc_ref, ur_ref, ua_ref, g0_ref, g1_ref, g2_ref, bg_ref,
                wc_ref, wr_ref, wa_ref, wo_ref, out_ref):
    def gate(g_ref, i):
        return _sigmoid(g_ref[...].astype(F32) + bg_ref[:, i * D_MODEL:(i + 1) * D_MODEL])

    merged = gate(g0_ref, 0) * jnp.dot(uc_ref[...], wc_ref[...], preferred_element_type=F32)
    merged = merged + gate(g1_ref, 1) * jnp.dot(ur_ref[...], wr_ref[...], preferred_element_type=F32)
    merged = merged + gate(g2_ref, 2) * jnp.dot(ua_ref[...], wa_ref[...], preferred_element_type=F32)
    out_ref[...] = x_ref[...] + jnp.dot(merged.astype(BF16), wo_ref[...], preferred_element_type=F32)


def _out_proj(x2, u_conv, u_ret, u_attn, z, b_gate, w_conv_out, w_ret_out, w_attn_out, w_out):
    n = x2.shape[0]

    def rows(width, col=0):
        return pl.BlockSpec((OUT_TM, width), lambda i: (i, col))

    def whole(shape):
        return pl.BlockSpec(shape, lambda i: (0, 0))

    return pl.pallas_call(
        _out_kernel,
        out_shape=jax.ShapeDtypeStruct((n, D_MODEL), F32),
        grid=(n // OUT_TM,),
        in_specs=[rows(D_MODEL), rows(D_MODEL), rows(2 * D_MODEL), rows(D_MODEL),
                  rows(D_MODEL, Z_MERGE), rows(D_MODEL, Z_MERGE + 1), rows(D_MODEL, Z_MERGE + 2),
                  whole((1, 3 * D_MODEL)),
                  whole((D_MODEL, D_MODEL)), whole((2 * D_MODEL, D_MODEL)),
                  whole((D_MODEL, D_MODEL)), whole((D_MODEL, D_MODEL))],
        out_specs=rows(D_MODEL),
        compiler_params=pltpu.CompilerParams(
            dimension_semantics=("arbitrary",), vmem_limit_bytes=VMEM_LIMIT_BYTES),
        name="out_proj",
    )(x2, u_conv, u_ret, u_attn, z, z, z, b_gate.reshape(1, 3 * D_MODEL),
      w_conv_out, w_ret_out, w_attn_out, w_out)


def _widen_w_in(w_in):
    w = w_in.astype(BF16)
    splits = [2048, 1024, 1024, 1024, 2048, 2048, 1024, 256, 256, 1024, 3072]
    offs = [0]
    for s in splits:
        offs.append(offs[-1] + s)
    seg = [w[:, offs[i]:offs[i + 1]] for i in range(len(splits))]

    def rep(wkv):
        wkv = wkv.reshape(D_MODEL, ATTN_KV_HEADS, 1, HEAD_DIM)
        return jnp.broadcast_to(wkv, (D_MODEL, ATTN_KV_HEADS, ATTN_GROUP, HEAD_DIM)).reshape(D_MODEL, D_MODEL)

    return jnp.concatenate(seg[:7] + [rep(seg[7]), rep(seg[8])] + seg[9:], axis=1)


def _rotary_tables(seq):
    pos = jnp.arange(seq, dtype=jnp.int32).astype(F32)
    half_r = RET_QK_DIM // 2
    inv_r = RET_THETA ** (-jnp.arange(half_r, dtype=F32) / half_r)
    ang_r = pos[:, None] * inv_r[None, :]
    half_a = ROPE_DIM // 2
    inv_a = ROPE_THETA ** (-jnp.arange(half_a, dtype=F32) / half_a)
    ang_a = pos[:, None] * inv_a[None, :]
    ca, sa = jnp.cos(ang_a), jnp.sin(ang_a)
    ones = jnp.ones((seq, HEAD_DIM - ROPE_DIM), F32)
    zeros = jnp.zeros((seq, HEAD_DIM - ROPE_DIM), F32)
    zh = jnp.zeros((seq, half_a), F32)
    c64 = jnp.concatenate([ca, ca, ones], axis=1)
    s_lo = jnp.concatenate([-sa, zh, zeros], axis=1)
    s_hi = jnp.concatenate([zh, sa, zeros], axis=1)
    rep = LANES // HEAD_DIM
    return (jnp.cos(ang_r), jnp.sin(ang_r),
            jnp.tile(c64, (1, rep)), jnp.tile(s_lo, (1, rep)), jnp.tile(s_hi, (1, rep)))


def kernel(x, norm_g, w_in, b_gate, conv_dw, conv_b, conv_ln_g, conv_ln_b, ret_decay,
           q_norm_g, k_norm_g, attn_sink, w_conv_out, w_ret_out, w_attn_out, w_out):
    batch, seq, d = x.shape
    depth = norm_g.shape[0]
    assert d == D_MODEL and seq % max(CONV_TILE, RET_CHUNK, PREP_TILE, ATTN_BLOCK) == 0
    assert (batch * seq) % PROJ_TM == 0
    cos_r, sin_r, rope_c, rope_s1, rope_s2 = _rotary_tables(seq)
    x2 = x.reshape(batch * seq, d)
    for l in range(depth):
        z = _in_proj(x2, norm_g[l], _widen_w_in(w_in[l]))
        u_conv = _conv_branch(z, seq, conv_dw[l], conv_b[l], conv_ln_g[l], conv_ln_b[l])
        u_ret = _ret_branch(z, batch, seq, ret_decay[l], cos_r, sin_r)
        qp, kp = _attn_prep(z, seq, rope_c, rope_s1, rope_s2, q_norm_g[l], k_norm_g[l])
        u_attn = _attn_branch(z, qp, kp, seq, attn_sink[l])
        x2 = _out_proj(x2, u_conv, u_ret, u_attn, z, b_gate[l],
                       w_conv_out[l].astype(BF16), w_ret_out[l].astype(BF16),
                       w_attn_out[l].astype(BF16), w_out[l].astype(BF16))
    return x2.reshape(batch, seq, d)
```

```python
import functools
import math

import jax
import jax.numpy as jnp
from jax import lax
from jax.experimental import pallas as pl
from jax.experimental.pallas import tpu as pltpu

F32 = jnp.float32
BF16 = jnp.bfloat16

D_MODEL = 1024
HEAD_DIM = 64
EPS = 1e-6
CONV_KERNEL = 31
CONV_HALF = CONV_KERNEL // 2
RET_HEADS = 4
RET_QK_DIM = 256
RET_V_DIM = 512
RET_THETA = 10000.0
ATTN_HEADS = 16
ATTN_KV_HEADS = 4
ATTN_GROUP = ATTN_HEADS // ATTN_KV_HEADS
ATTN_BLOCK = 128
ROPE_THETA = 500000.0
ROPE_DIM = HEAD_DIM // 4
KV_WIDTH = ATTN_KV_HEADS * HEAD_DIM
LOG2_E = math.log2(math.e)

LANES = 128
VMEM_LIMIT_BYTES = 56 * 1024 * 1024

Z_CONV_A, Z_CONV_B, Z_CONV_GATE = 0, 1024, 2048
Z_RET_Q, Z_RET_K, Z_RET_V, Z_RET_GATE = 3072, 4096, 5120, 7168
Z_ATT_Q, Z_ATT_GATE, Z_MERGE, Z_ATT_KV = 9216, 10240, 11264, 14336
Z_WIDTH = Z_ATT_KV + 2 * KV_WIDTH

PROJ_TM, PROJ_TN = 2048, 512
NORM_ROWS = 256
CONV_TILE, CONV_ROWS, CONV_HALO, CONV_TOKENS, CONV_NORM_UNROLL = 512, 32, 16, 16, 4
CBLK = D_MODEL // LANES
RET_CHUNK = 256
PREP_TILE = 512
OUT_TM = RET_CHUNK


def _sigmoid(x):
    return 1.0 / (1.0 + jnp.exp(-x))


def _silu(x):
    return x * _sigmoid(x)


def _in_proj_kernel(x_ref, g_ref, w_ref, z_ref, h_ref):
    @pl.when(pl.program_id(1) == 0)
    def _():
        def norm_rows(r, carry):
            rows = pl.ds(pl.multiple_of(r * NORM_ROWS, NORM_ROWS), NORM_ROWS)
            xv = x_ref[rows, :]
            ms = jnp.mean(xv * xv, axis=-1, keepdims=True)
            h_ref[rows, :] = (xv * lax.rsqrt(ms + EPS) * g_ref[...]).astype(BF16)
            return carry

        lax.fori_loop(0, PROJ_TM // NORM_ROWS, norm_rows, 0)

    z_ref[...] = jnp.dot(h_ref[...], w_ref[...], preferred_element_type=F32).astype(BF16)


def _in_proj(x2, norm_g, w):
    n = x2.shape[0]
    return pl.pallas_call(
        _in_proj_kernel,
        out_shape=jax.ShapeDtypeStruct((n, Z_WIDTH), BF16),
        grid=(n // PROJ_TM, Z_WIDTH // PROJ_TN),
        in_specs=[
            pl.BlockSpec((PROJ_TM, D_MODEL), lambda i, j: (i, 0)),
            pl.BlockSpec((1, D_MODEL), lambda i, j: (0, 0)),
            pl.BlockSpec((D_MODEL, PROJ_TN), lambda i, j: (0, j)),
        ],
        out_specs=pl.BlockSpec((PROJ_TM, PROJ_TN), lambda i, j: (i, j)),
        scratch_shapes=[pltpu.VMEM((PROJ_TM, D_MODEL), BF16)],
        compiler_params=pltpu.CompilerParams(
            dimension_semantics=("arbitrary", "arbitrary"), vmem_limit_bytes=VMEM_LIMIT_BYTES),
        name="in_proj",
    )(x2, norm_g.reshape(1, D_MODEL), w)


def _prep_kernel(aq_ref, akv_ref, rq_ref, rk_ref, cosr_ref, sinr_ref, cosa_ref, sina_ref,
                 gq_ref, gk_ref, avg_ref, swap_ref, rep_ref,
                 qp_ref, kp_ref, vp_ref, rqo_ref, rko_ref):
    cos_a = cosa_ref[...]
    sin_a = sina_ref[...]
    avg = avg_ref[...]
    swap = swap_ref[...]

    def norm_rope(x, g_ref, scale):
        ms = jnp.dot((x * x).astype(BF16), avg, preferred_element_type=F32)
        xg = x * g_ref[...]
        partner = jnp.dot(xg.astype(BF16), swap, preferred_element_type=F32)
        return (xg * cos_a + partner * sin_a) * (lax.rsqrt(ms + EPS) * scale)

    q_scale = LOG2_E * HEAD_DIM ** -0.5
    for c in range(D_MODEL // LANES):
        lanes = slice(c * LANES, (c + 1) * LANES)
        qp_ref[:, lanes] = norm_rope(aq_ref[:, lanes].astype(F32), gq_ref, q_scale).astype(BF16)
    k = jnp.concatenate(
        [norm_rope(akv_ref[:, c * LANES:(c + 1) * LANES].astype(F32), gk_ref, 1.0).astype(BF16)
         for c in range(KV_WIDTH // LANES)], axis=1)
    rep = rep_ref[...]
    kp_ref[...] = jnp.dot(k, rep, preferred_element_type=F32).astype(BF16)
    vp_ref[...] = jnp.dot(akv_ref[:, KV_WIDTH:], rep, preferred_element_type=F32).astype(BF16)

    cos_r = cosr_ref[...]
    sin_r = sinr_ref[...]
    half = RET_QK_DIM // 2
    for src, dst, scale in ((rq_ref, rqo_ref, 1.0), (rk_ref, rko_ref, RET_QK_DIM ** -0.5)):
        for h in range(RET_HEADS):
            lo = slice(h * RET_QK_DIM, h * RET_QK_DIM + half)
            hi = slice(h * RET_QK_DIM + half, (h + 1) * RET_QK_DIM)
            x1 = src[:, lo].astype(F32)
            x2 = src[:, hi].astype(F32)
            dst[:, lo] = ((x1 * cos_r - x2 * sin_r) * scale).astype(BF16)
            dst[:, hi] = ((x2 * cos_r + x1 * sin_r) * scale).astype(BF16)


def _prep(z, seq, cos_r, sin_r, cos_a, sin_a, q_norm_g, k_norm_g):
    n = z.shape[0]
    seq_tiles = seq // PREP_TILE
    heads_per_block = LANES // HEAD_DIM
    gq = jnp.tile(q_norm_g, heads_per_block).reshape(1, LANES)
    gk = jnp.tile(k_norm_g, heads_per_block).reshape(1, LANES)
    lane = jnp.arange(LANES)
    head = lane // HEAD_DIM
    avg = jnp.where(head[:, None] == head[None, :], 1.0 / HEAD_DIM, 0.0).astype(BF16)
    within = lane % HEAD_DIM
    half = ROPE_DIM // 2
    partner = jnp.where(within < half, lane + half, jnp.where(within < ROPE_DIM, lane - half, -1))
    swap = (lane[:, None] == partner[None, :]).astype(BF16)
    src = jnp.arange(KV_WIDTH)
    dst = jnp.arange(D_MODEL)
    rep = ((src[:, None] // HEAD_DIM == dst[None, :] // (ATTN_GROUP * HEAD_DIM))
           & (src[:, None] % HEAD_DIM == dst[None, :] % HEAD_DIM)).astype(BF16)

    def rows(width, col):
        return pl.BlockSpec((PREP_TILE, width), lambda i: (i, col // width))

    tab = pl.BlockSpec((PREP_TILE, LANES), lambda i: (i % seq_tiles, 0))
    vec = pl.BlockSpec((1, LANES), lambda i: (0, 0))
    out = jax.ShapeDtypeStruct((n, D_MODEL), BF16)
    out_spec = pl.BlockSpec((PREP_TILE, D_MODEL), lambda i: (i, 0))
    return pl.pallas_call(
        _prep_kernel,
        out_shape=(out,) * 5,
        grid=(n // PREP_TILE,),
        in_specs=[rows(D_MODEL, Z_ATT_Q), rows(2 * KV_WIDTH, Z_ATT_KV),
                  rows(D_MODEL, Z_RET_Q), rows(D_MODEL, Z_RET_K),
                  tab, tab, tab, tab, vec, vec,
                  pl.BlockSpec((LANES, LANES), lambda i: (0, 0)),
                  pl.BlockSpec((LANES, LANES), lambda i: (0, 0)),
                  pl.BlockSpec((KV_WIDTH, D_MODEL), lambda i: (0, 0))],
        out_specs=(out_spec,) * 5,
        compiler_params=pltpu.CompilerParams(
            dimension_semantics=("arbitrary",), vmem_limit_bytes=VMEM_LIMIT_BYTES),
        name="qk_prep",
    )(z, z, z, z, cos_r, sin_r, cos_a, sin_a, gq, gk, avg, swap, rep)


def _conv_kernel(seq_tiles, a_ref, b_ref, gate_ref, ap_ref, bp_ref, an_ref, bn_ref,
                 w_ref, cb_ref, lg_ref, lb_ref, out_ref, v_ref, y_ref):
    t = pl.program_id(0) % seq_tiles

    def glu(a, b):
        return a.astype(F32) * _sigmoid(b.astype(F32))

    def scatter(val, tok0):
        for c in range(CBLK):
            v_ref[pl.ds(tok0 * CBLK + c, val.shape[0], stride=CBLK), :] = val[:, c * LANES:(c + 1) * LANES]

    scatter(jnp.where(t == 0, 0.0, glu(ap_ref[...], bp_ref[...])), 0)
    scatter(jnp.where(t == seq_tiles - 1, 0.0, glu(an_ref[...], bn_ref[...])), CONV_HALO + CONV_TILE)

    def fill(r, carry):
        r0 = pl.multiple_of(r * CONV_ROWS, CONV_ROWS)
        rows = pl.ds(r0, CONV_ROWS)
        scatter(glu(a_ref[rows, :], b_ref[rows, :]), r0 + CONV_HALO)
        return carry

    lax.fori_loop(0, CONV_TILE // CONV_ROWS, fill, 0)

    bias = cb_ref[...]

    def conv_tokens(r, carry):
        t0 = r * CONV_TOKENS
        acc = [bias] * CONV_TOKENS
        for k in range(CONV_KERNEL):
            wk = w_ref[k * CBLK:(k + 1) * CBLK, :]
            for i in range(CONV_TOKENS):
                src = pl.multiple_of((t0 + (i + k - CONV_HALF + CONV_HALO)) * CBLK, CBLK)
                acc[i] = acc[i] + v_ref[pl.ds(src, CBLK), :] * wk
        for i in range(CONV_TOKENS):
            y_ref[pl.ds(pl.multiple_of((t0 + i) * CBLK, CBLK), CBLK), :] = acc[i]
        return carry

    lax.fori_loop(0, CONV_TILE // CONV_TOKENS, conv_tokens, 0)

    def norm_rows(r, carry):
        for u in range(CONV_NORM_UNROLL):
            r0 = pl.multiple_of((r * CONV_NORM_UNROLL + u) * CONV_ROWS, CONV_ROWS)
            y = jnp.concatenate(
                [y_ref[pl.ds(r0 * CBLK + c, CONV_ROWS, stride=CBLK), :] for c in range(CBLK)], axis=1)
            mu = jnp.mean(y, axis=-1, keepdims=True)
            d = y - mu
            var = jnp.mean(d * d, axis=-1, keepdims=True)
            yn = d * lax.rsqrt(var + EPS) * lg_ref[...] + lb_ref[...]
            g = gate_ref[pl.ds(r0, CONV_ROWS), :].astype(F32)
            out_ref[pl.ds(r0, CONV_ROWS), :] = (_silu(yn) * _silu(g)).astype(BF16)
        return carry

    lax.fori_loop(0, CONV_TILE // (CONV_ROWS * CONV_NORM_UNROLL), norm_rows, 0)


def _conv_branch(z, seq, conv_dw, conv_b, ln_g, ln_b):
    n = z.shape[0]
    seq_tiles = seq // CONV_TILE
    halo_per_tile = CONV_TILE // CONV_HALO
    last_halo = n // CONV_HALO - 1
    w_tiles = conv_dw.reshape(CONV_KERNEL * CBLK, LANES)

    def main(col):
        return pl.BlockSpec((CONV_TILE, D_MODEL), lambda i: (i, col // D_MODEL))

    def prev(col):
        return pl.BlockSpec((CONV_HALO, D_MODEL),
                            lambda i: (jnp.maximum(i * halo_per_tile - 1, 0), col // D_MODEL))

    def nxt(col):
        return pl.BlockSpec((CONV_HALO, D_MODEL),
                            lambda i: (jnp.minimum((i + 1) * halo_per_tile, last_halo), col // D_MODEL))

    vec = pl.BlockSpec((1, D_MODEL), lambda i: (0, 0))
    return pl.pallas_call(
        functools.partial(_conv_kernel, seq_tiles),
        out_shape=jax.ShapeDtypeStruct((n, D_MODEL), BF16),
        grid=(n // CONV_TILE,),
        in_specs=[main(Z_CONV_A), main(Z_CONV_B), main(Z_CONV_GATE),
                  prev(Z_CONV_A), prev(Z_CONV_B), nxt(Z_CONV_A), nxt(Z_CONV_B),
                  pl.BlockSpec((CONV_KERNEL * CBLK, LANES), lambda i: (0, 0)),
                  pl.BlockSpec((CBLK, LANES), lambda i: (0, 0)), vec, vec],
        out_specs=pl.BlockSpec((CONV_TILE, D_MODEL), lambda i: (i, 0)),
        scratch_shapes=[pltpu.VMEM(((CONV_TILE + 2 * CONV_HALO) * CBLK, LANES), F32),
                        pltpu.VMEM((CONV_TILE * CBLK, LANES), F32)],
        compiler_params=pltpu.CompilerParams(
            dimension_semantics=("arbitrary",), vmem_limit_bytes=VMEM_LIMIT_BYTES),
        name="conv_branch",
    )(z, z, z, z, z, z, z, w_tiles, conv_b.reshape(CBLK, LANES), ln_g.reshape(1, D_MODEL),
      ln_b.reshape(1, D_MODEL))


def _ret_kernel(n_chunks, rd_ref, qf_ref, kf_ref, vf_ref, qb_ref, kb_ref, vb_ref, gf_ref, gb_ref,
                out_ref, oacc_ref, sf_ref, sb_ref, dmat_ref, dec_ref, cdec_ref):
    L = RET_CHUNK
    head = pl.program_id(0) % RET_HEADS
    c = pl.program_id(1)

    @pl.when(c == 0)
    def _():
        idx = lax.broadcasted_iota(jnp.int32, (L, LANES), 0).astype(F32)
        lg_f = -jnp.exp(jnp.full((L, LANES), rd_ref[0, head], F32))
        lg_b = -jnp.exp(jnp.full((L, LANES), rd_ref[1, head], F32))
        dec_ref[0] = jnp.exp(lg_f * (idx + 1.0))
        dec_ref[1] = jnp.exp(lg_f * (L - 1.0 - idx))
        dec_ref[2] = jnp.exp(lg_b * (L - idx))
        dec_ref[3] = jnp.exp(lg_b * idx)
        cdec_ref[0] = jnp.exp(lg_f[:RET_QK_DIM] * float(L))
        cdec_ref[1] = jnp.exp(lg_b[:RET_QK_DIM] * float(L))
        sf_ref[...] = jnp.zeros_like(sf_ref)
        sb_ref[...] = jnp.zeros_like(sb_ref)
        lgm_f = -jnp.exp(jnp.full((L, L), rd_ref[0, head], F32))
        lgm_b = -jnp.exp(jnp.full((L, L), rd_ref[1, head], F32))
        i = lax.broadcasted_iota(jnp.int32, (L, L), 0)
        j = lax.broadcasted_iota(jnp.int32, (L, L), 1)
        d = (i - j).astype(F32)
        dmat_ref[...] = jnp.where(i >= j, jnp.exp(lgm_f * jnp.maximum(d, 0.0)),
                                  jnp.exp(lgm_b * jnp.maximum(-d, 0.0)))

    def decayed(ref, dec):
        return (ref[...].astype(F32) * jnp.concatenate([dec] * (RET_QK_DIM // LANES), axis=1)).astype(BF16)

    def advance(state_ref, q_ref, k_ref, v_ref, q_dec, k_dec, chunk_dec):
        state = state_ref[...]
        cross = jnp.dot(decayed(q_ref, q_dec), state.astype(BF16), preferred_element_type=F32)
        kv = lax.dot_general(decayed(k_ref, k_dec), v_ref[...], (((0,), (0,)), ((), ())),
                             preferred_element_type=F32)
        state_ref[...] = jnp.concatenate([chunk_dec] * (RET_V_DIM // LANES), axis=1) * state + kv
        return cross

    s = lax.dot_general(qf_ref[...], kf_ref[...], (((1,), (1,)), ((), ())), preferred_element_type=F32)
    part_f = jnp.dot((s * dmat_ref[...]).astype(BF16), vf_ref[...], preferred_element_type=F32)
    part_f = part_f + advance(sf_ref, qf_ref, kf_ref, vf_ref, dec_ref[0], dec_ref[1], cdec_ref[0])
    part_b = advance(sb_ref, qb_ref, kb_ref, vb_ref, dec_ref[2], dec_ref[3], cdec_ref[1])
    rows_f = pl.ds(pl.multiple_of(c * L, L), L)
    rows_b = pl.ds(pl.multiple_of((n_chunks - 1 - c) * L, L), L)

    @pl.when(c < n_chunks // 2)
    def _():
        oacc_ref[rows_f, :] = part_f
        oacc_ref[rows_b, :] = part_b

    def finish(o, gate_ref):
        mu = jnp.mean(o, axis=-1, keepdims=True)
        d = o - mu
        var = jnp.mean(d * d, axis=-1, keepdims=True)
        return (d * lax.rsqrt(var + EPS) * _silu(gate_ref[...].astype(F32))).astype(BF16)

    @pl.when(c >= n_chunks // 2)
    def _():
        out_ref[0] = finish(oacc_ref[rows_f, :] + part_f, gf_ref)
        out_ref[1] = finish(oacc_ref[rows_b, :] + part_b, gb_ref)


def _ret_branch(z, rq, rk, batch, seq, ret_decay):
    L = RET_CHUNK
    n_chunks = seq // L
    half = n_chunks // 2
    assert n_chunks % 2 == 0

    def fwd(bh, c):
        return (bh // RET_HEADS) * n_chunks + c

    def bwd(bh, c):
        return (bh // RET_HEADS) * n_chunks + (n_chunks - 1 - c)

    def head(bh):
        return bh % RET_HEADS

    def zcol(col, width, bh):
        return col // width + bh % RET_HEADS

    qk = (L, RET_QK_DIM)
    vv = (L, RET_V_DIM)
    in_specs = [
        pl.BlockSpec(memory_space=pltpu.SMEM),
        pl.BlockSpec(qk, lambda bh, c: (fwd(bh, c), head(bh))),
        pl.BlockSpec(qk, lambda bh, c: (fwd(bh, c), head(bh))),
        pl.BlockSpec(vv, lambda bh, c: (fwd(bh, c), zcol(Z_RET_V, RET_V_DIM, bh))),
        pl.BlockSpec(qk, lambda bh, c: (bwd(bh, c), head(bh))),
        pl.BlockSpec(qk, lambda bh, c: (bwd(bh, c), head(bh))),
        pl.BlockSpec(vv, lambda bh, c: (bwd(bh, c), zcol(Z_RET_V, RET_V_DIM, bh))),
        pl.BlockSpec(vv, lambda bh, c: (fwd(bh, jnp.maximum(c, half)), zcol(Z_RET_GATE, RET_V_DIM, bh))),
        pl.BlockSpec(vv, lambda bh, c: (bwd(bh, jnp.maximum(c, half)), zcol(Z_RET_GATE, RET_V_DIM, bh))),
    ]
    return pl.pallas_call(
        functools.partial(_ret_kernel, n_chunks),
        out_shape=jax.ShapeDtypeStruct((batch, half, 2, L, RET_HEADS * RET_V_DIM), BF16),
        grid=(batch * RET_HEADS, n_chunks),
        in_specs=in_specs,
        out_specs=pl.BlockSpec((None, None, 2, L, RET_V_DIM),
                               lambda bh, c: (bh // RET_HEADS, jnp.maximum(c - half, 0), 0, 0, head(bh))),
        scratch_shapes=[
            pltpu.VMEM((seq, RET_V_DIM), F32),
            pltpu.VMEM((RET_QK_DIM, RET_V_DIM), F32),
            pltpu.VMEM((RET_QK_DIM, RET_V_DIM), F32),
            pltpu.VMEM((L, L), F32),
            pltpu.VMEM((4, L, LANES), F32),
            pltpu.VMEM((2, RET_QK_DIM, LANES), F32),
        ],
        compiler_params=pltpu.CompilerParams(
            dimension_semantics=("arbitrary", "arbitrary"), vmem_limit_bytes=VMEM_LIMIT_BYTES),
        name="retention",
    )(ret_decay, rq, rk, z, rq, rk, z, z, z)


def _attn_kernel(seq_blocks, sink_ref, q_ref, km_ref, kr_ref, vm_ref, vr_ref, gate_ref, out_ref,
                 kring_ref, vring_ref):
    T = ATTN_BLOCK
    GW = ATTN_GROUP * HEAD_DIM
    blk = pl.program_id(0) % seq_blocks
    lane = lax.broadcasted_iota(jnp.int32, (T, GW), 1)
    head_masks = [jnp.where(lane // HEAD_DIM == h, 1.0, 0.0).astype(BF16) for h in range(ATTN_GROUP)]

    def stack(k_ref, v_ref, slot):
        for g in range(ATTN_KV_HEADS):
            lanes = slice(g * GW, (g + 1) * GW)
            kring_ref[slot, g] = jnp.concatenate([k_ref[:, lanes] * m for m in head_masks], axis=0)
            vring_ref[slot, g] = jnp.concatenate([v_ref[:, lanes] * m for m in head_masks], axis=0)

    @pl.when(blk == 0)
    def _():
        kring_ref[2] = jnp.zeros(kring_ref.shape[1:], BF16)
        vring_ref[2] = jnp.zeros(vring_ref.shape[1:], BF16)
        stack(km_ref, vm_ref, 0)

    @pl.when(blk < seq_blocks - 1)
    def _():
        stack(kr_ref, vr_ref, (blk + 1) % 3)

    slots = ((blk + 2) % 3, blk % 3, (blk + 1) % 3)
    row = lax.broadcasted_iota(jnp.int32, (T, T), 0)
    col = lax.broadcasted_iota(jnp.int32, (T, T), 1)
    mask_l = (col >= row) & (blk > 0)
    mask_r = (col <= row) & (blk < seq_blocks - 1)
    lane_o = lax.broadcasted_iota(jnp.int32, (T, LANES), 1)

    for g in range(ATTN_KV_HEADS):
        lanes = slice(g * GW, (g + 1) * GW)
        qg = q_ref[:, lanes]
        scores = [lax.dot_general(qg, kring_ref[slot, g], (((1,), (1,)), ((), ())),
                                  preferred_element_type=F32) for slot in slots]
        probs = [[], [], []]
        inv_l = []
        for h in range(ATTN_GROUP):
            hs = slice(h * T, (h + 1) * T)
            sl = jnp.where(mask_l, scores[0][:, hs], -1e30)
            sm = scores[1][:, hs]
            sr = jnp.where(mask_r, scores[2][:, hs], -1e30)
            sink = sink_ref[g * ATTN_GROUP + h] * LOG2_E
            m = jnp.maximum(jnp.max(jnp.maximum(jnp.maximum(sl, sm), sr), axis=-1, keepdims=True), sink)
            el = jnp.exp2(sl - m)
            em = jnp.exp2(sm - m)
            er = jnp.exp2(sr - m)
            denom = jnp.sum(el + em + er, axis=-1, keepdims=True) + jnp.exp2(sink - m)
            inv_l.append(1.0 / denom)
            probs[0].append(el.astype(BF16))
            probs[1].append(em.astype(BF16))
            probs[2].append(er.astype(BF16))
        o = jnp.zeros((T, GW), F32)
        for w, slot in enumerate(slots):
            o = o + jnp.dot(jnp.concatenate(probs[w], axis=1), vring_ref[slot, g],
                            preferred_element_type=F32)
        norm = []
        for hb in range(GW // LANES):
            lo = jnp.broadcast_to(inv_l[2 * hb], (T, LANES))
            hi = jnp.broadcast_to(inv_l[2 * hb + 1], (T, LANES))
            norm.append(jnp.where(lane_o < HEAD_DIM, lo, hi))
        gate = gate_ref[:, lanes].astype(F32)
        out_ref[:, lanes] = (o * jnp.concatenate(norm, axis=1) * _silu(gate)).astype(BF16)


def _attn_branch(z, qp, kp, vp, seq, attn_sink):
    n = z.shape[0]
    T = ATTN_BLOCK
    seq_blocks = seq // T
    last = n // T - 1
    blk = (T, D_MODEL)

    def here(col=0):
        return pl.BlockSpec(blk, lambda i: (i, col // D_MODEL))

    right = pl.BlockSpec(blk, lambda i: (jnp.minimum(i + 1, last), 0))
    ring = pltpu.VMEM((3, ATTN_KV_HEADS, ATTN_GROUP * T, ATTN_GROUP * HEAD_DIM), BF16)
    return pl.pallas_call(
        functools.partial(_attn_kernel, seq_blocks),
        out_shape=jax.ShapeDtypeStruct((n, D_MODEL), BF16),
        grid=(n // T,),
        in_specs=[pl.BlockSpec(memory_space=pltpu.SMEM),
                  here(), here(), right, here(), right, here(Z_ATT_GATE)],
        out_specs=here(),
        scratch_shapes=[ring, ring],
        compiler_params=pltpu.CompilerParams(
            dimension_semantics=("arbitrary",), vmem_limit_bytes=VMEM_LIMIT_BYTES),
        name="window_attn",
    )(attn_sink, qp, kp, kp, vp, vp, z)


def _out_kernel(x_ref, uc_ref, ur_ref, ua_ref, g0_ref, g1_ref, g2_ref, bg_ref,
                wc_ref, wr_ref, wa_ref, wo_ref, out_ref):
    def gate(g_ref, i):
        return _sigmoid(g_ref[...].astype(F32) + bg_ref[:, i * D_MODEL:(i + 1) * D_MODEL])

    merged = gate(g0_ref, 0) * jnp.dot(uc_ref[...], wc_ref[...], preferred_element_type=F32)
    merged = merged + gate(g1_ref, 1) * jnp.dot(ur_ref[...], wr_ref[...], preferred_element_type=F32)
    merged = merged + gate(g2_ref, 2) * jnp.dot(ua_ref[...], wa_ref[...], preferred_element_type=F32)
    out_ref[...] = x_ref[...] + jnp.dot(merged.astype(BF16), wo_ref[...], preferred_element_type=F32)


def _out_proj(x2, u_conv, u_ret, u_attn, z, seq, b_gate, w_conv_out, w_ret_out, w_attn_out, w_out):
    n = x2.shape[0]
    n_chunks = seq // OUT_TM
    half = n_chunks // 2

    def rows(width, col=0):
        return pl.BlockSpec((OUT_TM, width), lambda i: (i, col // width))

    def whole(shape):
        return pl.BlockSpec(shape, lambda i: (0, 0))

    def ret_index(i):
        b, c = i // n_chunks, i % n_chunks
        upper = c >= half
        return (b, jnp.where(upper, c - half, half - 1 - c), jnp.where(upper, 0, 1), 0, 0)

    return pl.pallas_call(
        _out_kernel,
        out_shape=jax.ShapeDtypeStruct((n, D_MODEL), F32),
        grid=(n // OUT_TM,),
        in_specs=[rows(D_MODEL), rows(D_MODEL),
                  pl.BlockSpec((None, None, None, OUT_TM, 2 * D_MODEL), ret_index),
                  rows(D_MODEL),
                  rows(D_MODEL, Z_MERGE), rows(D_MODEL, Z_MERGE + D_MODEL), rows(D_MODEL, Z_MERGE + 2 * D_MODEL),
                  whole((1, 3 * D_MODEL)),
                  whole((D_MODEL, D_MODEL)), whole((2 * D_MODEL, D_MODEL)),
                  whole((D_MODEL, D_MODEL)), whole((D_MODEL, D_MODEL))],
        out_specs=rows(D_MODEL),
        compiler_params=pltpu.CompilerParams(
            dimension_semantics=("arbitrary",), vmem_limit_bytes=VMEM_LIMIT_BYTES),
        name="out_proj",
    )(x2, u_conv, u_ret, u_attn, z, z, z, b_gate.reshape(1, 3 * D_MODEL),
      w_conv_out, w_ret_out, w_attn_out, w_out)


def _reorder_w_in(w_in):
    w = w_in.astype(BF16)
    kv0 = 9216 + D_MODEL
    return jnp.concatenate([w[:, :kv0], w[:, kv0 + 2 * KV_WIDTH:], w[:, kv0:kv0 + 2 * KV_WIDTH]], axis=1)


def _rotary_tables(seq):
    pos = jnp.arange(seq, dtype=jnp.int32).astype(F32)
    half_r = RET_QK_DIM // 2
    inv_r = RET_THETA ** (-jnp.arange(half_r, dtype=F32) / half_r)
    ang_r = pos[:, None] * inv_r[None, :]
    half_a = ROPE_DIM // 2
    inv_a = ROPE_THETA ** (-jnp.arange(half_a, dtype=F32) / half_a)
    ang_a = pos[:, None] * inv_a[None, :]
    ca, sa = jnp.cos(ang_a), jnp.sin(ang_a)
    rest = HEAD_DIM - ROPE_DIM
    cos64 = jnp.concatenate([ca, ca, jnp.ones((seq, rest), F32)], axis=1)
    sin64 = jnp.concatenate([-sa, sa, jnp.zeros((seq, rest), F32)], axis=1)
    rep = LANES // HEAD_DIM
    return jnp.cos(ang_r), jnp.sin(ang_r), jnp.tile(cos64, (1, rep)), jnp.tile(sin64, (1, rep))


def kernel(x, norm_g, w_in, b_gate, conv_dw, conv_b, conv_ln_g, conv_ln_b, ret_decay,
           q_norm_g, k_norm_g, attn_sink, w_conv_out, w_ret_out, w_attn_out, w_out):
    batch, seq, d = x.shape
    depth = norm_g.shape[0]
    assert d == D_MODEL and seq % max(CONV_TILE, 2 * RET_CHUNK, PREP_TILE, ATTN_BLOCK) == 0
    assert (batch * seq) % PROJ_TM == 0 and w_in.shape[-1] == Z_WIDTH
    cos_r, sin_r, cos_a, sin_a = _rotary_tables(seq)
    x2 = x.reshape(batch * seq, d)
    for l in range(depth):
        z = _in_proj(x2, norm_g[l], _reorder_w_in(w_in[l]))
        qp, kp, vp, rq, rk = _prep(z, seq, cos_r, sin_r, cos_a, sin_a, q_norm_g[l], k_norm_g[l])
        u_conv = _conv_branch(z, seq, conv_dw[l], conv_b[l], conv_ln_g[l], conv_ln_b[l])
        u_ret = _ret_branch(z, rq, rk, batch, seq, ret_decay[l])
        u_attn = _attn_branch(z, qp, kp, vp, seq, attn_sink[l])
        x2 = _out_proj(x2, u_conv, u_ret, u_attn, z, seq, b_gate[l],
                       w_conv_out[l].astype(BF16), w_ret_out[l].astype(BF16),
                       w_attn_out[l].astype(BF16), w_out[l].astype(BF16))
    return x2.reshape(batch, seq, d)
```

```python
import functools
import math

import jax
import jax.numpy as jnp
from jax import lax
from jax.experimental import pallas as pl
from jax.experimental.pallas import tpu as pltpu

F32 = jnp.float32
BF16 = jnp.bfloat16

D_MODEL = 1024
HEAD_DIM = 64
EPS = 1e-6
CONV_KERNEL = 31
CONV_HALF = CONV_KERNEL // 2
RET_HEADS = 4
RET_QK_DIM = 256
RET_V_DIM = 512
RET_THETA = 10000.0
ATTN_HEADS = 16
ATTN_KV_HEADS = 4
ATTN_GROUP = ATTN_HEADS // ATTN_KV_HEADS
ATTN_BLOCK = 128
ROPE_THETA = 500000.0
ROPE_DIM = HEAD_DIM // 4
KV_WIDTH = ATTN_KV_HEADS * HEAD_DIM
LOG2_E = math.log2(math.e)

LANES = 128
VMEM_LIMIT_BYTES = 56 * 1024 * 1024

Z_CONV_A, Z_CONV_B, Z_CONV_GATE = 0, 1024, 2048
Z_RET_Q, Z_RET_K, Z_RET_V, Z_RET_GATE = 3072, 4096, 5120, 7168
Z_ATT_Q, Z_ATT_GATE, Z_MERGE, Z_ATT_KV = 9216, 10240, 11264, 14336
Z_WIDTH = Z_ATT_KV + 2 * KV_WIDTH
REF_ATT_KV = Z_ATT_Q + D_MODEL

PROJ_TM, PROJ_TN = 2048, 512
NORM_ROWS = 256
CONV_TILE, CONV_ROWS, CONV_HALO, CONV_TOKENS, CONV_NORM_UNROLL = 512, 32, 16, 16, 4
CBLK = D_MODEL // LANES
RET_CHUNK = 256
PREP_TILE, PREP_ROWS = 512, 128
V_REP_WIDTH = ATTN_KV_HEADS * LANES
OUT_TM = RET_CHUNK


def _sigmoid(x):
    return 1.0 / (1.0 + jnp.exp(-x))


def _silu(x):
    return x * _sigmoid(x)


def _in_proj_kernel(x_ref, g_ref, w_ref, z_ref, h_ref):
    @pl.when(pl.program_id(1) == 0)
    def _():
        def norm_rows(r, carry):
            rows = pl.ds(pl.multiple_of(r * NORM_ROWS, NORM_ROWS), NORM_ROWS)
            xv = x_ref[rows, :]
            ms = jnp.mean(xv * xv, axis=-1, keepdims=True)
            h_ref[rows, :] = (xv * lax.rsqrt(ms + EPS) * g_ref[...]).astype(BF16)
            return carry

        lax.fori_loop(0, PROJ_TM // NORM_ROWS, norm_rows, 0)

    z_ref[...] = jnp.dot(h_ref[...], w_ref[...].astype(BF16), preferred_element_type=F32).astype(BF16)


def _in_proj(x2, norm_g, w_in, layer):
    n = x2.shape[0]
    depth = norm_g.shape[0]
    kv_tile = REF_ATT_KV // PROJ_TN
    n_tiles = Z_WIDTH // PROJ_TN
    assert REF_ATT_KV % PROJ_TN == 0 and 2 * KV_WIDTH == PROJ_TN

    def w_tile(j):
        return jnp.where(j == n_tiles - 1, kv_tile, jnp.where(j >= kv_tile, j + 1, j))

    return pl.pallas_call(
        _in_proj_kernel,
        out_shape=jax.ShapeDtypeStruct((n, Z_WIDTH), BF16),
        grid=(n // PROJ_TM, n_tiles),
        in_specs=[
            pl.BlockSpec((PROJ_TM, D_MODEL), lambda i, j: (i, 0)),
            pl.BlockSpec((None, 1, D_MODEL), lambda i, j: (layer, 0, 0)),
            pl.BlockSpec((None, D_MODEL, PROJ_TN), lambda i, j: (layer, 0, w_tile(j))),
        ],
        out_specs=pl.BlockSpec((PROJ_TM, PROJ_TN), lambda i, j: (i, j)),
        scratch_shapes=[pltpu.VMEM((PROJ_TM, D_MODEL), BF16)],
        compiler_params=pltpu.CompilerParams(
            dimension_semantics=("arbitrary", "arbitrary"), vmem_limit_bytes=VMEM_LIMIT_BYTES),
        name="in_proj",
    )(x2, norm_g.reshape(depth, 1, D_MODEL), w_in)


def _prep_kernel(aq_ref, akv_ref, rq_ref, rk_ref, cosr_ref, sinr_ref, cosa_ref, sina_ref,
                 gq_ref, gk_ref, avg_ref, swap_ref, repk_ref, repv_ref,
                 qp_ref, kp_ref, vp_ref, rqo_ref, rko_ref, k_ref):
    avg = avg_ref[...]
    swap = swap_ref[...]
    q_scale = LOG2_E * HEAD_DIM ** -0.5
    half = RET_QK_DIM // 2

    def chunk(r, carry):
        rows = pl.ds(pl.multiple_of(r * PREP_ROWS, PREP_ROWS), PREP_ROWS)
        cos_a = cosa_ref[rows, :]
        sin_a = sina_ref[rows, :]

        def norm_rope(x, g_ref, scale):
            ms = jnp.dot((x * x).astype(BF16), avg, preferred_element_type=F32)
            xg = x * g_ref[...]
            partner = jnp.dot(xg.astype(BF16), swap, preferred_element_type=F32)
            return (xg * cos_a + partner * sin_a) * (lax.rsqrt(ms + EPS) * scale)

        for c in range(D_MODEL // LANES):
            lanes = slice(c * LANES, (c + 1) * LANES)
            qp_ref[rows, lanes] = norm_rope(aq_ref[rows, lanes].astype(F32), gq_ref, q_scale).astype(BF16)
        for c in range(KV_WIDTH // LANES):
            lanes = slice(c * LANES, (c + 1) * LANES)
            k_ref[rows, lanes] = norm_rope(akv_ref[rows, lanes].astype(F32), gk_ref, 1.0).astype(BF16)

        cos_r = cosr_ref[rows, :]
        sin_r = sinr_ref[rows, :]
        for src, dst, scale in ((rq_ref, rqo_ref, 1.0), (rk_ref, rko_ref, RET_QK_DIM ** -0.5)):
            for h in range(RET_HEADS):
                lo = slice(h * RET_QK_DIM, h * RET_QK_DIM + half)
                hi = slice(h * RET_QK_DIM + half, (h + 1) * RET_QK_DIM)
                x1 = src[rows, lo].astype(F32)
                x2 = src[rows, hi].astype(F32)
                dst[rows, lo] = ((x1 * cos_r - x2 * sin_r) * scale).astype(BF16)
                dst[rows, hi] = ((x2 * cos_r + x1 * sin_r) * scale).astype(BF16)
        return carry

    lax.fori_loop(0, PREP_TILE // PREP_ROWS, chunk, 0)
    kp_ref[...] = jnp.dot(k_ref[...], repk_ref[...], preferred_element_type=F32).astype(BF16)
    vp_ref[...] = jnp.dot(akv_ref[:, KV_WIDTH:], repv_ref[...], preferred_element_type=F32).astype(BF16)


def _prep(z, seq, cos_r, sin_r, cos_a, sin_a, q_norm_g, k_norm_g):
    n = z.shape[0]
    seq_tiles = seq // PREP_TILE
    heads_per_block = LANES // HEAD_DIM
    gq = jnp.tile(q_norm_g, heads_per_block).reshape(1, LANES)
    gk = jnp.tile(k_norm_g, heads_per_block).reshape(1, LANES)
    lane = jnp.arange(LANES)
    head = lane // HEAD_DIM
    avg = jnp.where(head[:, None] == head[None, :], 1.0 / HEAD_DIM, 0.0).astype(BF16)
    within = lane % HEAD_DIM
    half = ROPE_DIM // 2
    partner = jnp.where(within < half, lane + half, jnp.where(within < ROPE_DIM, lane - half, -1))
    swap = (lane[:, None] == partner[None, :]).astype(BF16)
    src = jnp.arange(KV_WIDTH)

    def rep(slots):
        dst = jnp.arange(ATTN_KV_HEADS * slots * HEAD_DIM)
        return ((src[:, None] // HEAD_DIM == dst[None, :] // (slots * HEAD_DIM))
                & (src[:, None] % HEAD_DIM == dst[None, :] % HEAD_DIM)).astype(BF16)

    def rows(width, col):
        return pl.BlockSpec((PREP_TILE, width), lambda i: (i, col // width))

    def whole(shape):
        return pl.BlockSpec(shape, lambda i: (0, 0))

    tab = pl.BlockSpec((PREP_TILE, LANES), lambda i: (i % seq_tiles, 0))
    wide = jax.ShapeDtypeStruct((n, D_MODEL), BF16)
    wide_spec = pl.BlockSpec((PREP_TILE, D_MODEL), lambda i: (i, 0))
    return pl.pallas_call(
        _prep_kernel,
        out_shape=(wide, wide, jax.ShapeDtypeStruct((n, V_REP_WIDTH), BF16), wide, wide),
        grid=(n // PREP_TILE,),
        in_specs=[rows(D_MODEL, Z_ATT_Q), rows(2 * KV_WIDTH, Z_ATT_KV),
                  rows(D_MODEL, Z_RET_Q), rows(D_MODEL, Z_RET_K),
                  tab, tab, tab, tab, whole((1, LANES)), whole((1, LANES)),
                  whole((LANES, LANES)), whole((LANES, LANES)),
                  whole((KV_WIDTH, D_MODEL)), whole((KV_WIDTH, V_REP_WIDTH))],
        out_specs=(wide_spec, wide_spec, pl.BlockSpec((PREP_TILE, V_REP_WIDTH), lambda i: (i, 0)),
                   wide_spec, wide_spec),
        scratch_shapes=[pltpu.VMEM((PREP_TILE, KV_WIDTH), BF16)],
        compiler_params=pltpu.CompilerParams(
            dimension_semantics=("arbitrary",), vmem_limit_bytes=VMEM_LIMIT_BYTES),
        name="qk_prep",
    )(z, z, z, z, cos_r, sin_r, cos_a, sin_a, gq, gk, avg, swap, rep(ATTN_GROUP), rep(LANES // HEAD_DIM))


def _conv_kernel(seq_tiles, a_ref, b_ref, gate_ref, ap_ref, bp_ref, an_ref, bn_ref,
                 w_ref, cb_ref, lg_ref, lb_ref, out_ref, v_ref, y_ref):
    t = pl.program_id(0) % seq_tiles

    def glu(a, b):
        return a.astype(F32) * _sigmoid(b.astype(F32))

    def scatter(val, tok0):
        for c in range(CBLK):
            v_ref[pl.ds(tok0 * CBLK + c, val.shape[0], stride=CBLK), :] = val[:, c * LANES:(c + 1) * LANES]

    scatter(jnp.where(t == 0, 0.0, glu(ap_ref[...], bp_ref[...])), 0)
    scatter(jnp.where(t == seq_tiles - 1, 0.0, glu(an_ref[...], bn_ref[...])), CONV_HALO + CONV_TILE)

    def fill(r, carry):
        r0 = pl.multiple_of(r * CONV_ROWS, CONV_ROWS)
        rows = pl.ds(r0, CONV_ROWS)
        scatter(glu(a_ref[rows, :], b_ref[rows, :]), r0 + CONV_HALO)
        return carry

    lax.fori_loop(0, CONV_TILE // CONV_ROWS, fill, 0)

    bias = cb_ref[...]

    def conv_tokens(r, carry):
        t0 = r * CONV_TOKENS
        acc = [bias] * CONV_TOKENS
        for k in range(CONV_KERNEL):
            wk = w_ref[k * CBLK:(k + 1) * CBLK, :]
            for i in range(CONV_TOKENS):
                src = pl.multiple_of((t0 + (i + k - CONV_HALF + CONV_HALO)) * CBLK, CBLK)
                acc[i] = acc[i] + v_ref[pl.ds(src, CBLK), :] * wk
        for i in range(CONV_TOKENS):
            y_ref[pl.ds(pl.multiple_of((t0 + i) * CBLK, CBLK), CBLK), :] = acc[i]
        return carry

    lax.fori_loop(0, CONV_TILE // CONV_TOKENS, conv_tokens, 0)

    def norm_rows(r, carry):
        for u in range(CONV_NORM_UNROLL):
            r0 = pl.multiple_of((r * CONV_NORM_UNROLL + u) * CONV_ROWS, CONV_ROWS)
            y = jnp.concatenate(
                [y_ref[pl.ds(r0 * CBLK + c, CONV_ROWS, stride=CBLK), :] for c in range(CBLK)], axis=1)
            mu = jnp.mean(y, axis=-1, keepdims=True)
            d = y - mu
            var = jnp.mean(d * d, axis=-1, keepdims=True)
            yn = d * lax.rsqrt(var + EPS) * lg_ref[...] + lb_ref[...]
            g = gate_ref[pl.ds(r0, CONV_ROWS), :].astype(F32)
            out_ref[pl.ds(r0, CONV_ROWS), :] = (_silu(yn) * _silu(g)).astype(BF16)
        return carry

    lax.fori_loop(0, CONV_TILE // (CONV_ROWS * CONV_NORM_UNROLL), norm_rows, 0)


def _conv_branch(z, seq, conv_dw, conv_b, ln_g, ln_b):
    n = z.shape[0]
    seq_tiles = seq // CONV_TILE
    halo_per_tile = CONV_TILE // CONV_HALO
    last_halo = n // CONV_HALO - 1
    w_tiles = conv_dw.reshape(CONV_KERNEL * CBLK, LANES)

    def main(col):
        return pl.BlockSpec((CONV_TILE, D_MODEL), lambda i: (i, col // D_MODEL))

    def prev(col):
        return pl.BlockSpec((CONV_HALO, D_MODEL),
                            lambda i: (jnp.maximum(i * halo_per_tile - 1, 0), col // D_MODEL))

    def nxt(col):
        return pl.BlockSpec((CONV_HALO, D_MODEL),
                            lambda i: (jnp.minimum((i + 1) * halo_per_tile, last_halo), col // D_MODEL))

    vec = pl.BlockSpec((1, D_MODEL), lambda i: (0, 0))
    return pl.pallas_call(
        functools.partial(_conv_kernel, seq_tiles),
        out_shape=jax.ShapeDtypeStruct((n, D_MODEL), BF16),
        grid=(n // CONV_TILE,),
        in_specs=[main(Z_CONV_A), main(Z_CONV_B), main(Z_CONV_GATE),
                  prev(Z_CONV_A), prev(Z_CONV_B), nxt(Z_CONV_A), nxt(Z_CONV_B),
                  pl.BlockSpec((CONV_KERNEL * CBLK, LANES), lambda i: (0, 0)),
                  pl.BlockSpec((CBLK, LANES), lambda i: (0, 0)), vec, vec],
        out_specs=pl.BlockSpec((CONV_TILE, D_MODEL), lambda i: (i, 0)),
        scratch_shapes=[pltpu.VMEM(((CONV_TILE + 2 * CONV_HALO) * CBLK, LANES), F32),
                        pltpu.VMEM((CONV_TILE * CBLK, LANES), F32)],
        compiler_params=pltpu.CompilerParams(
            dimension_semantics=("arbitrary",), vmem_limit_bytes=VMEM_LIMIT_BYTES),
        name="conv_branch",
    )(z, z, z, z, z, z, z, w_tiles, conv_b.reshape(CBLK, LANES), ln_g.reshape(1, D_MODEL),
      ln_b.reshape(1, D_MODEL))


def _ret_kernel(n_chunks, rd_ref, qf_ref, kf_ref, vf_ref, qb_ref, kb_ref, vb_ref, gf_ref, gb_ref,
                out_ref, oacc_ref, sf_ref, sb_ref, dmat_ref, dec_ref, cdec_ref):
    L = RET_CHUNK
    head = pl.program_id(0) % RET_HEADS
    c = pl.program_id(1)

    @pl.when(c == 0)
    def _():
        idx = lax.broadcasted_iota(jnp.int32, (L, LANES), 0).astype(F32)
        lg_f = -jnp.exp(jnp.full((L, LANES), rd_ref[0, head], F32))
        lg_b = -jnp.exp(jnp.full((L, LANES), rd_ref[1, head], F32))
        dec_ref[0] = jnp.exp(lg_f * (idx + 1.0))
        dec_ref[1] = jnp.exp(lg_f * (L - 1.0 - idx))
        dec_ref[2] = jnp.exp(lg_b * (L - idx))
        dec_ref[3] = jnp.exp(lg_b * idx)
        cdec_ref[0] = jnp.exp(lg_f[:RET_QK_DIM] * float(L))
        cdec_ref[1] = jnp.exp(lg_b[:RET_QK_DIM] * float(L))
        sf_ref[...] = jnp.zeros_like(sf_ref)
        sb_ref[...] = jnp.zeros_like(sb_ref)
        lgm_f = -jnp.exp(jnp.full((L, L), rd_ref[0, head], F32))
        lgm_b = -jnp.exp(jnp.full((L, L), rd_ref[1, head], F32))
        i = lax.broadcasted_iota(jnp.int32, (L, L), 0)
        j = lax.broadcasted_iota(jnp.int32, (L, L), 1)
        d = (i - j).astype(F32)
        dmat_ref[...] = jnp.where(i >= j, jnp.exp(lgm_f * jnp.maximum(d, 0.0)),
                                  jnp.exp(lgm_b * jnp.maximum(-d, 0.0)))

    def decayed(ref, dec):
        return (ref[...].astype(F32) * jnp.concatenate([dec] * (RET_QK_DIM // LANES), axis=1)).astype(BF16)

    def advance(state_ref, q_ref, k_ref, v_ref, q_dec, k_dec, chunk_dec):
        state = state_ref[...]
        cross = jnp.dot(decayed(q_ref, q_dec), state.astype(BF16), preferred_element_type=F32)
        kv = lax.dot_general(decayed(k_ref, k_dec), v_ref[...], (((0,), (0,)), ((), ())),
                             preferred_element_type=F32)
        state_ref[...] = jnp.concatenate([chunk_dec] * (RET_V_DIM // LANES), axis=1) * state + kv
        return cross

    s = lax.dot_general(qf_ref[...], kf_ref[...], (((1,), (1,)), ((), ())), preferred_element_type=F32)
    part_f = jnp.dot((s * dmat_ref[...]).astype(BF16), vf_ref[...], preferred_element_type=F32)
    part_f = part_f + advance(sf_ref, qf_ref, kf_ref, vf_ref, dec_ref[0], dec_ref[1], cdec_ref[0])
    part_b = advance(sb_ref, qb_ref, kb_ref, vb_ref, dec_ref[2], dec_ref[3], cdec_ref[1])
    rows_f = pl.ds(pl.multiple_of(c * L, L), L)
    rows_b = pl.ds(pl.multiple_of((n_chunks - 1 - c) * L, L), L)

    @pl.when(c < n_chunks // 2)
    def _():
        oacc_ref[rows_f, :] = part_f
        oacc_ref[rows_b, :] = part_b

    def finish(o, gate_ref):
        mu = jnp.mean(o, axis=-1, keepdims=True)
        d = o - mu
        var = jnp.mean(d * d, axis=-1, keepdims=True)
        return (d * lax.rsqrt(var + EPS) * _silu(gate_ref[...].astype(F32))).astype(BF16)

    @pl.when(c >= n_chunks // 2)
    def _():
        out_ref[0] = finish(oacc_ref[rows_f, :] + part_f, gf_ref)
        out_ref[1] = finish(oacc_ref[rows_b, :] + part_b, gb_ref)


def _ret_branch(z, rq, rk, batch, seq, ret_decay):
    L = RET_CHUNK
    n_chunks = seq // L
    half = n_chunks // 2
    assert n_chunks % 2 == 0

    def fwd(bh, c):
        return (bh // RET_HEADS) * n_chunks + c

    def bwd(bh, c):
        return (bh // RET_HEADS) * n_chunks + (n_chunks - 1 - c)

    def head(bh):
        return bh % RET_HEADS

    def zcol(col, width, bh):
        return col // width + bh % RET_HEADS

    qk = (L, RET_QK_DIM)
    vv = (L, RET_V_DIM)
    in_specs = [
        pl.BlockSpec(memory_space=pltpu.SMEM),
        pl.BlockSpec(qk, lambda bh, c: (fwd(bh, c), head(bh))),
        pl.BlockSpec(qk, lambda bh, c: (fwd(bh, c), head(bh))),
        pl.BlockSpec(vv, lambda bh, c: (fwd(bh, c), zcol(Z_RET_V, RET_V_DIM, bh))),
        pl.BlockSpec(qk, lambda bh, c: (bwd(bh, c), head(bh))),
        pl.BlockSpec(qk, lambda bh, c: (bwd(bh, c), head(bh))),
        pl.BlockSpec(vv, lambda bh, c: (bwd(bh, c), zcol(Z_RET_V, RET_V_DIM, bh))),
        pl.BlockSpec(vv, lambda bh, c: (fwd(bh, jnp.maximum(c, half)), zcol(Z_RET_GATE, RET_V_DIM, bh))),
        pl.BlockSpec(vv, lambda bh, c: (bwd(bh, jnp.maximum(c, half)), zcol(Z_RET_GATE, RET_V_DIM, bh))),
    ]
    return pl.pallas_call(
        functools.partial(_ret_kernel, n_chunks),
        out_shape=jax.ShapeDtypeStruct((batch, half, 2, L, RET_HEADS * RET_V_DIM), BF16),
        grid=(batch * RET_HEADS, n_chunks),
        in_specs=in_specs,
        out_specs=pl.BlockSpec((None, None, 2, L, RET_V_DIM),
                               lambda bh, c: (bh // RET_HEADS, jnp.maximum(c - half, 0), 0, 0, head(bh))),
        scratch_shapes=[
            pltpu.VMEM((seq, RET_V_DIM), F32),
            pltpu.VMEM((RET_QK_DIM, RET_V_DIM), F32),
            pltpu.VMEM((RET_QK_DIM, RET_V_DIM), F32),
            pltpu.VMEM((L, L), F32),
            pltpu.VMEM((4, L, LANES), F32),
            pltpu.VMEM((2, RET_QK_DIM, LANES), F32),
        ],
        compiler_params=pltpu.CompilerParams(
            dimension_semantics=("arbitrary", "arbitrary"), vmem_limit_bytes=VMEM_LIMIT_BYTES),
        name="retention",
    )(ret_decay, rq, rk, z, rq, rk, z, z, z)


def _attn_kernel(seq_blocks, sink_ref, q_ref, kl_ref, km_ref, kr_ref, vl_ref, vm_ref, vr_ref,
                 gate_ref, out_ref):
    T = ATTN_BLOCK
    GW = ATTN_GROUP * HEAD_DIM
    blk = pl.program_id(0) % seq_blocks
    lane = lax.broadcasted_iota(jnp.int32, (T, GW), 1)
    head_masks = [jnp.where(lane // HEAD_DIM == h, 1.0, 0.0).astype(BF16) for h in range(ATTN_GROUP)]
    row = lax.broadcasted_iota(jnp.int32, (T, T), 0)
    col = lax.broadcasted_iota(jnp.int32, (T, T), 1)
    mask_l = (col >= row) & (blk > 0)
    mask_r = (col <= row) & (blk < seq_blocks - 1)
    lane_o = lax.broadcasted_iota(jnp.int32, (T, LANES), 1)

    def scores(g):
        lanes = slice(g * GW, (g + 1) * GW)
        qg = q_ref[:, lanes]
        q_heads = jnp.concatenate([qg * m for m in head_masks], axis=0)
        k_win = jnp.concatenate([kl_ref[:, lanes], km_ref[:, lanes], kr_ref[:, lanes]], axis=0)
        return lax.dot_general(q_heads, k_win, (((1,), (1,)), ((), ())), preferred_element_type=F32)

    def softmax(g, s):
        ps, invs = [], []
        for h in range(ATTN_GROUP):
            rows = slice(h * T, (h + 1) * T)
            sl = jnp.where(mask_l, s[rows, :T], -1e30)
            sm = s[rows, T:2 * T]
            sr = jnp.where(mask_r, s[rows, 2 * T:], -1e30)
            sink = sink_ref[g * ATTN_GROUP + h] * LOG2_E
            m = jnp.maximum(jnp.max(jnp.maximum(jnp.maximum(sl, sm), sr), axis=-1, keepdims=True), sink)
            el = jnp.exp2(sl - m)
            em = jnp.exp2(sm - m)
            er = jnp.exp2(sr - m)
            invs.append(1.0 / (jnp.sum(el + em + er, axis=-1, keepdims=True) + jnp.exp2(sink - m)))
            ps.append(jnp.concatenate([el, em, er], axis=1).astype(BF16))
        return ps, invs

    def values(g, ps, invs):
        lanes = slice(g * GW, (g + 1) * GW)
        outs = []
        v_win = jnp.concatenate([v_ref[:, g * LANES:(g + 1) * LANES] for v_ref in (vl_ref, vm_ref, vr_ref)],
                                axis=0)
        for hb in range(GW // LANES):
            o2 = jnp.dot(jnp.concatenate(ps[2 * hb:2 * hb + 2], axis=0), v_win, preferred_element_type=F32)
            outs.append(jnp.where(lane_o < HEAD_DIM, o2[:T] * invs[2 * hb], o2[T:] * invs[2 * hb + 1]))
        gate = gate_ref[:, lanes].astype(F32)
        out_ref[:, lanes] = (jnp.concatenate(outs, axis=1) * _silu(gate)).astype(BF16)

    ss = [scores(g) for g in range(ATTN_KV_HEADS)]
    ps = [softmax(g, ss[g]) for g in range(ATTN_KV_HEADS)]
    for g in range(ATTN_KV_HEADS):
        values(g, *ps[g])


def _attn_branch(z, qp, kp, vp, seq, attn_sink):
    n = z.shape[0]
    T = ATTN_BLOCK
    seq_blocks = seq // T
    last = n // T - 1
    blk = (T, D_MODEL)

    def here(col=0):
        return pl.BlockSpec(blk, lambda i: (i, col // D_MODEL))

    def window(width):
        return [pl.BlockSpec((T, width), lambda i: (jnp.maximum(i - 1, 0), 0)),
                pl.BlockSpec((T, width), lambda i: (i, 0)),
                pl.BlockSpec((T, width), lambda i: (jnp.minimum(i + 1, last), 0))]

    return pl.pallas_call(
        functools.partial(_attn_kernel, seq_blocks),
        out_shape=jax.ShapeDtypeStruct((n, D_MODEL), BF16),
        grid=(n // T,),
        in_specs=[pl.BlockSpec(memory_space=pltpu.SMEM), here()] + window(D_MODEL) + window(V_REP_WIDTH)
        + [here(Z_ATT_GATE)],
        out_specs=here(),
        compiler_params=pltpu.CompilerParams(
            dimension_semantics=("arbitrary",), vmem_limit_bytes=VMEM_LIMIT_BYTES),
        name="window_attn",
    )(attn_sink, qp, kp, kp, kp, vp, vp, vp, z)


def _out_kernel(x_ref, uc_ref, ur_ref, ua_ref, g0_ref, g1_ref, g2_ref, bg_ref,
                wc_ref, wr_ref, wa_ref, wo_ref, out_ref):
    def gate(g_ref, i):
        return _sigmoid(g_ref[...].astype(F32) + bg_ref[:, i * D_MODEL:(i + 1) * D_MODEL])

    merged = gate(g0_ref, 0) * jnp.dot(uc_ref[...], wc_ref[...], preferred_element_type=F32)
    merged = merged + gate(g1_ref, 1) * jnp.dot(ur_ref[...], wr_ref[...], preferred_element_type=F32)
    merged = merged + gate(g2_ref, 2) * jnp.dot(ua_ref[...], wa_ref[...], preferred_element_type=F32)
    out_ref[...] = x_ref[...] + jnp.dot(merged.astype(BF16), wo_ref[...], preferred_element_type=F32)


def _out_proj(x2, u_conv, u_ret, u_attn, z, seq, b_gate, w_conv_out, w_ret_out, w_attn_out, w_out, layer):
    n = x2.shape[0]
    n_chunks = seq // OUT_TM
    half = n_chunks // 2

    def rows(width, col=0):
        return pl.BlockSpec((OUT_TM, width), lambda i: (i, col // width))

    def whole(shape):
        return pl.BlockSpec((None,) + shape, lambda i: (layer, 0, 0))

    def ret_index(i):
        b, c = i // n_chunks, i % n_chunks
        upper = c >= half
        return (b, jnp.where(upper, c - half, half - 1 - c), jnp.where(upper, 0, 1), 0, 0)

    return pl.pallas_call(
        _out_kernel,
        out_shape=jax.ShapeDtypeStruct((n, D_MODEL), F32),
        grid=(n // OUT_TM,),
        in_specs=[rows(D_MODEL), rows(D_MODEL),
                  pl.BlockSpec((None, None, None, OUT_TM, 2 * D_MODEL), ret_index),
                  rows(D_MODEL),
                  rows(D_MODEL, Z_MERGE), rows(D_MODEL, Z_MERGE + D_MODEL), rows(D_MODEL, Z_MERGE + 2 * D_MODEL),
                  whole((1, 3 * D_MODEL)),
                  whole((D_MODEL, D_MODEL)), whole((2 * D_MODEL, D_MODEL)),
                  whole((D_MODEL, D_MODEL)), whole((D_MODEL, D_MODEL))],
        out_specs=rows(D_MODEL),
        compiler_params=pltpu.CompilerParams(
            dimension_semantics=("arbitrary",), vmem_limit_bytes=VMEM_LIMIT_BYTES),
        name="out_proj",
    )(x2, u_conv, u_ret, u_attn, z, z, z, b_gate.reshape(-1, 1, 3 * D_MODEL),
      w_conv_out, w_ret_out, w_attn_out, w_out)


def _rotary_tables(seq):
    pos = jnp.arange(seq, dtype=jnp.int32).astype(F32)
    half_r = RET_QK_DIM // 2
    inv_r = RET_THETA ** (-jnp.arange(half_r, dtype=F32) / half_r)
    ang_r = pos[:, None] * inv_r[None, :]
    half_a = ROPE_DIM // 2
    inv_a = ROPE_THETA ** (-jnp.arange(half_a, dtype=F32) / half_a)
    ang_a = pos[:, None] * inv_a[None, :]
    ca, sa = jnp.cos(ang_a), jnp.sin(ang_a)
    rest = HEAD_DIM - ROPE_DIM
    cos64 = jnp.concatenate([ca, ca, jnp.ones((seq, rest), F32)], axis=1)
    sin64 = jnp.concatenate([-sa, sa, jnp.zeros((seq, rest), F32)], axis=1)
    rep = LANES // HEAD_DIM
    return jnp.cos(ang_r), jnp.sin(ang_r), jnp.tile(cos64, (1, rep)), jnp.tile(sin64, (1, rep))


def kernel(x, norm_g, w_in, b_gate, conv_dw, conv_b, conv_ln_g, conv_ln_b, ret_decay,
           q_norm_g, k_norm_g, attn_sink, w_conv_out, w_ret_out, w_attn_out, w_out):
    batch, seq, d = x.shape
    depth = norm_g.shape[0]
    assert d == D_MODEL and seq % max(CONV_TILE, 2 * RET_CHUNK, PREP_TILE, ATTN_BLOCK) == 0
    assert (batch * seq) % PROJ_TM == 0 and w_in.shape[-1] == Z_WIDTH
    cos_r, sin_r, cos_a, sin_a = _rotary_tables(seq)
    x2 = x.reshape(batch * seq, d)
    out_weights = [w.astype(BF16) for w in (w_conv_out, w_ret_out, w_attn_out, w_out)]
    for l in range(depth):
        z = _in_proj(x2, norm_g, w_in, l)
        qp, kp, vp, rq, rk = _prep(z, seq, cos_r, sin_r, cos_a, sin_a, q_norm_g[l], k_norm_g[l])
        u_conv = _conv_branch(z, seq, conv_dw[l], conv_b[l], conv_ln_g[l], conv_ln_b[l])
        u_ret = _ret_branch(z, rq, rk, batch, seq, ret_decay[l])
        u_attn = _attn_branch(z, qp, kp, vp, seq, attn_sink[l])
        x2 = _out_proj(x2, u_conv, u_ret, u_attn, z, seq, b_gate, *out_weights, l)
    return x2.reshape(batch, seq, d)
```

```python
import functools
import math

import jax
import jax.numpy as jnp
from jax import lax
from jax.experimental import pallas as pl
from jax.experimental.pallas import tpu as pltpu

F32 = jnp.float32
BF16 = jnp.bfloat16

D_MODEL = 1024
HEAD_DIM = 64
EPS = 1e-6
CONV_KERNEL = 31
CONV_HALF = CONV_KERNEL // 2
RET_HEADS = 4
RET_QK_DIM = 256
RET_V_DIM = 512
RET_THETA = 10000.0
ATTN_HEADS = 16
ATTN_KV_HEADS = 4
ATTN_GROUP = ATTN_HEADS // ATTN_KV_HEADS
ATTN_BLOCK = 128
ROPE_THETA = 500000.0
ROPE_DIM = HEAD_DIM // 4
KV_WIDTH = ATTN_KV_HEADS * HEAD_DIM
LOG2_E = math.log2(math.e)

LANES = 128
VMEM_LIMIT_BYTES = 56 * 1024 * 1024

Z_CONV_A, Z_CONV_B, Z_CONV_GATE = 0, 1024, 2048
Z_RET_Q, Z_RET_K, Z_RET_V, Z_RET_GATE = 3072, 4096, 5120, 7168
Z_ATT_Q, Z_ATT_GATE, Z_MERGE, Z_ATT_KV = 9216, 10240, 11264, 14336
Z_WIDTH = Z_ATT_KV + 2 * KV_WIDTH
REF_ATT_KV = Z_ATT_Q + D_MODEL

PROJ_TM, PROJ_TN = 2048, 512
NORM_ROWS = 256
CONV_TILE, CONV_ROWS, CONV_HALO, CONV_TOKENS, CONV_NORM_UNROLL = 512, 32, 16, 16, 4
CBLK = D_MODEL // LANES
RET_CHUNK = 256
RET_HPS = 2
PREP_TILE, PREP_ROWS = 512, 128
V_REP_WIDTH = ATTN_KV_HEADS * LANES
OUT_TM = RET_CHUNK
OUT_NORM_ROWS = 64


def _sigmoid(x):
    return 1.0 / (1.0 + jnp.exp(-x))


def _silu(x):
    return x * _sigmoid(x)


def _in_proj_kernel(x_ref, g_ref, w_ref, cos_ref, sin_ref, z_ref, h_ref):
    j = pl.program_id(1)

    @pl.when(j == 0)
    def _():
        def norm_rows(r, carry):
            rows = pl.ds(pl.multiple_of(r * NORM_ROWS, NORM_ROWS), NORM_ROWS)
            xv = x_ref[rows, :]
            ms = jnp.mean(xv * xv, axis=-1, keepdims=True)
            h_ref[rows, :] = (xv * lax.rsqrt(ms + EPS) * g_ref[...]).astype(BF16)
            return carry

        lax.fori_loop(0, PROJ_TM // NORM_ROWS, norm_rows, 0)

    def product():
        return jnp.dot(h_ref[...], w_ref[...].astype(BF16), preferred_element_type=F32)

    rotary_tile = (j >= Z_RET_Q // PROJ_TN) & (j < Z_RET_V // PROJ_TN)

    @pl.when(jnp.logical_not(rotary_tile))
    def _():
        z_ref[...] = product().astype(BF16)

    @pl.when(rotary_tile)
    def _():
        acc = product()
        scale = jnp.where(j >= Z_RET_K // PROJ_TN, RET_QK_DIM ** -0.5, 1.0)
        c = cos_ref[...] * scale
        s = sin_ref[...] * scale
        half = RET_QK_DIM // 2
        for hd in range(PROJ_TN // RET_QK_DIM):
            lo = slice(hd * RET_QK_DIM, hd * RET_QK_DIM + half)
            hi = slice(hd * RET_QK_DIM + half, (hd + 1) * RET_QK_DIM)
            z_ref[:, lo] = (acc[:, lo] * c - acc[:, hi] * s).astype(BF16)
            z_ref[:, hi] = (acc[:, hi] * c + acc[:, lo] * s).astype(BF16)


def _in_proj(x2, norm_g, w_in, cos_r, sin_r, layer):
    n = x2.shape[0]
    depth = norm_g.shape[0]
    seq_tiles = cos_r.shape[0] // PROJ_TM
    table = pl.BlockSpec((PROJ_TM, LANES), lambda i, j: (i % seq_tiles, 0))
    kv_tile = REF_ATT_KV // PROJ_TN
    n_tiles = Z_WIDTH // PROJ_TN
    assert REF_ATT_KV % PROJ_TN == 0 and 2 * KV_WIDTH == PROJ_TN

    def w_tile(j):
        return jnp.where(j == n_tiles - 1, kv_tile, jnp.where(j >= kv_tile, j + 1, j))

    return pl.pallas_call(
        _in_proj_kernel,
        out_shape=jax.ShapeDtypeStruct((n, Z_WIDTH), BF16),
        grid=(n // PROJ_TM, n_tiles),
        in_specs=[
            pl.BlockSpec((PROJ_TM, D_MODEL), lambda i, j: (i, 0)),
            pl.BlockSpec((None, 1, D_MODEL), lambda i, j: (layer, 0, 0)),
            pl.BlockSpec((None, D_MODEL, PROJ_TN), lambda i, j: (layer, 0, w_tile(j))),
            table, table,
        ],
        out_specs=pl.BlockSpec((PROJ_TM, PROJ_TN), lambda i, j: (i, j)),
        scratch_shapes=[pltpu.VMEM((PROJ_TM, D_MODEL), BF16)],
        compiler_params=pltpu.CompilerParams(
            dimension_semantics=("arbitrary", "arbitrary"), vmem_limit_bytes=VMEM_LIMIT_BYTES),
        name="in_proj",
    )(x2, norm_g.reshape(depth, 1, D_MODEL), w_in, cos_r, sin_r)


def _prep_kernel(aq_ref, akv_ref, cosa_ref, sina_ref, gq_ref, gk_ref, avg_ref, swap_ref,
                 repk_ref, repv_ref, qp_ref, kp_ref, vp_ref, k_ref):
    avg = avg_ref[...]
    swap = swap_ref[...]
    q_scale = LOG2_E * HEAD_DIM ** -0.5

    def chunk(r, carry):
        rows = pl.ds(pl.multiple_of(r * PREP_ROWS, PREP_ROWS), PREP_ROWS)
        cos_a = cosa_ref[rows, :]
        sin_a = sina_ref[rows, :]

        def norm_rope(x, g_ref, scale):
            ms = jnp.dot((x * x).astype(BF16), avg, preferred_element_type=F32)
            xg = x * g_ref[...]
            partner = jnp.dot(xg.astype(BF16), swap, preferred_element_type=F32)
            return (xg * cos_a + partner * sin_a) * (lax.rsqrt(ms + EPS) * scale)

        for c in range(D_MODEL // LANES):
            lanes = slice(c * LANES, (c + 1) * LANES)
            qp_ref[rows, lanes] = norm_rope(aq_ref[rows, lanes].astype(F32), gq_ref, q_scale).astype(BF16)
        for c in range(KV_WIDTH // LANES):
            lanes = slice(c * LANES, (c + 1) * LANES)
            k_ref[rows, lanes] = norm_rope(akv_ref[rows, lanes].astype(F32), gk_ref, 1.0).astype(BF16)
        return carry

    lax.fori_loop(0, PREP_TILE // PREP_ROWS, chunk, 0)
    kp_ref[...] = jnp.dot(k_ref[...], repk_ref[...], preferred_element_type=F32).astype(BF16)
    vp_ref[...] = jnp.dot(akv_ref[:, KV_WIDTH:], repv_ref[...], preferred_element_type=F32).astype(BF16)


def _prep(z, seq, cos_a, sin_a, q_norm_g, k_norm_g):
    n = z.shape[0]
    seq_tiles = seq // PREP_TILE
    heads_per_block = LANES // HEAD_DIM
    gq = jnp.tile(q_norm_g, heads_per_block).reshape(1, LANES)
    gk = jnp.tile(k_norm_g, heads_per_block).reshape(1, LANES)
    lane = jnp.arange(LANES)
    head = lane // HEAD_DIM
    avg = jnp.where(head[:, None] == head[None, :], 1.0 / HEAD_DIM, 0.0).astype(BF16)
    within = lane % HEAD_DIM
    half = ROPE_DIM // 2
    partner = jnp.where(within < half, lane + half, jnp.where(within < ROPE_DIM, lane - half, -1))
    swap = (lane[:, None] == partner[None, :]).astype(BF16)
    src = jnp.arange(KV_WIDTH)

    def rep(slots):
        dst = jnp.arange(ATTN_KV_HEADS * slots * HEAD_DIM)
        return ((src[:, None] // HEAD_DIM == dst[None, :] // (slots * HEAD_DIM))
                & (src[:, None] % HEAD_DIM == dst[None, :] % HEAD_DIM)).astype(BF16)

    def rows(width, col):
        return pl.BlockSpec((PREP_TILE, width), lambda i: (i, col // width))

    def whole(shape):
        return pl.BlockSpec(shape, lambda i: (0, 0))

    tab = pl.BlockSpec((PREP_TILE, LANES), lambda i: (i % seq_tiles, 0))
    wide = jax.ShapeDtypeStruct((n, D_MODEL), BF16)
    wide_spec = pl.BlockSpec((PREP_TILE, D_MODEL), lambda i: (i, 0))
    return pl.pallas_call(
        _prep_kernel,
        out_shape=(wide, wide, jax.ShapeDtypeStruct((n, V_REP_WIDTH), BF16)),
        grid=(n // PREP_TILE,),
        in_specs=[rows(D_MODEL, Z_ATT_Q), rows(2 * KV_WIDTH, Z_ATT_KV),
                  tab, tab, whole((1, LANES)), whole((1, LANES)),
                  whole((LANES, LANES)), whole((LANES, LANES)),
                  whole((KV_WIDTH, D_MODEL)), whole((KV_WIDTH, V_REP_WIDTH))],
        out_specs=(wide_spec, wide_spec, pl.BlockSpec((PREP_TILE, V_REP_WIDTH), lambda i: (i, 0))),
        scratch_shapes=[pltpu.VMEM((PREP_TILE, KV_WIDTH), BF16)],
        compiler_params=pltpu.CompilerParams(
            dimension_semantics=("arbitrary",), vmem_limit_bytes=VMEM_LIMIT_BYTES),
        name="qk_prep",
    )(z, z, cos_a, sin_a, gq, gk, avg, swap, rep(ATTN_GROUP), rep(LANES // HEAD_DIM))


def _conv_kernel(seq_tiles, a_ref, b_ref, gate_ref, ap_ref, bp_ref, an_ref, bn_ref,
                 w_ref, cb_ref, lg_ref, lb_ref, out_ref, v_ref, y_ref):
    t = pl.program_id(0) % seq_tiles

    def glu(a, b):
        return a.astype(F32) * _sigmoid(b.astype(F32))

    def scatter(val, tok0):
        for c in range(CBLK):
            v_ref[pl.ds(tok0 * CBLK + c, val.shape[0], stride=CBLK), :] = val[:, c * LANES:(c + 1) * LANES]

    scatter(jnp.where(t == 0, 0.0, glu(ap_ref[...], bp_ref[...])), 0)
    scatter(jnp.where(t == seq_tiles - 1, 0.0, glu(an_ref[...], bn_ref[...])), CONV_HALO + CONV_TILE)

    def fill(r, carry):
        r0 = pl.multiple_of(r * CONV_ROWS, CONV_ROWS)
        rows = pl.ds(r0, CONV_ROWS)
        scatter(glu(a_ref[rows, :], b_ref[rows, :]), r0 + CONV_HALO)
        return carry

    lax.fori_loop(0, CONV_TILE // CONV_ROWS, fill, 0)

    bias = cb_ref[...]

    def conv_tokens(r, carry):
        t0 = r * CONV_TOKENS
        acc = [bias] * CONV_TOKENS
        for k in range(CONV_KERNEL):
            wk = w_ref[k * CBLK:(k + 1) * CBLK, :]
            for i in range(CONV_TOKENS):
                src = pl.multiple_of((t0 + (i + k - CONV_HALF + CONV_HALO)) * CBLK, CBLK)
                acc[i] = acc[i] + v_ref[pl.ds(src, CBLK), :] * wk
        for i in range(CONV_TOKENS):
            y_ref[pl.ds(pl.multiple_of((t0 + i) * CBLK, CBLK), CBLK), :] = acc[i]
        return carry

    lax.fori_loop(0, CONV_TILE // CONV_TOKENS, conv_tokens, 0)

    def norm_rows(r, carry):
        for u in range(CONV_NORM_UNROLL):
            r0 = pl.multiple_of((r * CONV_NORM_UNROLL + u) * CONV_ROWS, CONV_ROWS)
            y = jnp.concatenate(
                [y_ref[pl.ds(r0 * CBLK + c, CONV_ROWS, stride=CBLK), :] for c in range(CBLK)], axis=1)
            mu = jnp.mean(y, axis=-1, keepdims=True)
            d = y - mu
            var = jnp.mean(d * d, axis=-1, keepdims=True)
            yn = d * lax.rsqrt(var + EPS) * lg_ref[...] + lb_ref[...]
            g = gate_ref[pl.ds(r0, CONV_ROWS), :].astype(F32)
            out_ref[pl.ds(r0, CONV_ROWS), :] = (_silu(yn) * _silu(g)).astype(BF16)
        return carry

    lax.fori_loop(0, CONV_TILE // (CONV_ROWS * CONV_NORM_UNROLL), norm_rows, 0)


def _conv_branch(z, seq, conv_dw, conv_b, ln_g, ln_b):
    n = z.shape[0]
    seq_tiles = seq // CONV_TILE
    halo_per_tile = CONV_TILE // CONV_HALO
    last_halo = n // CONV_HALO - 1
    w_tiles = conv_dw.reshape(CONV_KERNEL * CBLK, LANES)

    def main(col):
        return pl.BlockSpec((CONV_TILE, D_MODEL), lambda i: (i, col // D_MODEL))

    def prev(col):
        return pl.BlockSpec((CONV_HALO, D_MODEL),
                            lambda i: (jnp.maximum(i * halo_per_tile - 1, 0), col // D_MODEL))

    def nxt(col):
        return pl.BlockSpec((CONV_HALO, D_MODEL),
                            lambda i: (jnp.minimum((i + 1) * halo_per_tile, last_halo), col // D_MODEL))

    vec = pl.BlockSpec((1, D_MODEL), lambda i: (0, 0))
    return pl.pallas_call(
        functools.partial(_conv_kernel, seq_tiles),
        out_shape=jax.ShapeDtypeStruct((n, D_MODEL), BF16),
        grid=(n // CONV_TILE,),
        in_specs=[main(Z_CONV_A), main(Z_CONV_B), main(Z_CONV_GATE),
                  prev(Z_CONV_A), prev(Z_CONV_B), nxt(Z_CONV_A), nxt(Z_CONV_B),
                  pl.BlockSpec((CONV_KERNEL * CBLK, LANES), lambda i: (0, 0)),
                  pl.BlockSpec((CBLK, LANES), lambda i: (0, 0)), vec, vec],
        out_specs=pl.BlockSpec((CONV_TILE, D_MODEL), lambda i: (i, 0)),
        scratch_shapes=[pltpu.VMEM(((CONV_TILE + 2 * CONV_HALO) * CBLK, LANES), F32),
                        pltpu.VMEM((CONV_TILE * CBLK, LANES), F32)],
        compiler_params=pltpu.CompilerParams(
            dimension_semantics=("arbitrary",), vmem_limit_bytes=VMEM_LIMIT_BYTES),
        name="conv_branch",
    )(z, z, z, z, z, z, z, w_tiles, conv_b.reshape(CBLK, LANES), ln_g.reshape(1, D_MODEL),
      ln_b.reshape(1, D_MODEL))


def _ret_kernel(n_chunks, rd_ref, qf_ref, kf_ref, vf_ref, qb_ref, kb_ref, vb_ref,
                out_ref, oacc_ref, state_ref, dmat_ref, dec_ref, cdec_ref):
    L = RET_CHUNK
    head0 = (pl.program_id(0) % (RET_HEADS // RET_HPS)) * RET_HPS
    c = pl.program_id(1)

    @pl.when(c == 0)
    def _():
        idx = lax.broadcasted_iota(jnp.int32, (L, LANES), 0).astype(F32)
        i = lax.broadcasted_iota(jnp.int32, (L, L), 0)
        j = lax.broadcasted_iota(jnp.int32, (L, L), 1)
        d = (i - j).astype(F32)
        state_ref[...] = jnp.zeros_like(state_ref)
        for hd in range(RET_HPS):
            lg_f = -jnp.exp(jnp.full((L, LANES), rd_ref[0, head0 + hd], F32))
            lg_b = -jnp.exp(jnp.full((L, LANES), rd_ref[1, head0 + hd], F32))
            dec_ref[hd, 0] = jnp.exp(lg_f * (idx + 1.0))
            dec_ref[hd, 1] = jnp.exp(lg_f * (L - 1.0 - idx))
            dec_ref[hd, 2] = jnp.exp(lg_b * (L - idx))
            dec_ref[hd, 3] = jnp.exp(lg_b * idx)
            cdec_ref[hd, 0] = jnp.exp(lg_f[:RET_QK_DIM] * float(L))
            cdec_ref[hd, 1] = jnp.exp(lg_b[:RET_QK_DIM] * float(L))
            lgm_f = -jnp.exp(jnp.full((L, L), rd_ref[0, head0 + hd], F32))
            lgm_b = -jnp.exp(jnp.full((L, L), rd_ref[1, head0 + hd], F32))
            dmat_ref[hd] = jnp.where(i >= j, jnp.exp(lgm_f * jnp.maximum(d, 0.0)),
                                     jnp.exp(lgm_b * jnp.maximum(-d, 0.0)))

    rows_f = pl.ds(pl.multiple_of(c * L, L), L)
    rows_b = pl.ds(pl.multiple_of((n_chunks - 1 - c) * L, L), L)
    parts = []
    for hd in range(RET_HPS):
        qk_cols = slice(hd * RET_QK_DIM, (hd + 1) * RET_QK_DIM)
        v_cols = slice(hd * RET_V_DIM, (hd + 1) * RET_V_DIM)

        def decayed(ref, dec):
            return (ref[:, qk_cols].astype(F32)
                    * jnp.concatenate([dec] * (RET_QK_DIM // LANES), axis=1)).astype(BF16)

        def advance(direction, q_ref, k_ref, v_ref):
            state = state_ref[hd, direction]
            cross = jnp.dot(decayed(q_ref, dec_ref[hd, 2 * direction]), state.astype(BF16),
                            preferred_element_type=F32)
            kv = lax.dot_general(decayed(k_ref, dec_ref[hd, 2 * direction + 1]), v_ref[:, v_cols],
                                 (((0,), (0,)), ((), ())), preferred_element_type=F32)
            chunk_dec = jnp.concatenate([cdec_ref[hd, direction]] * (RET_V_DIM // LANES), axis=1)
            state_ref[hd, direction] = chunk_dec * state + kv
            return cross

        s = lax.dot_general(qf_ref[:, qk_cols], kf_ref[:, qk_cols], (((1,), (1,)), ((), ())),
                            preferred_element_type=F32)
        part_f = jnp.dot((s * dmat_ref[hd]).astype(BF16), vf_ref[:, v_cols], preferred_element_type=F32)
        part_f = part_f + advance(0, qf_ref, kf_ref, vf_ref)
        part_b = advance(1, qb_ref, kb_ref, vb_ref)
        parts.append((v_cols, part_f, part_b))

    @pl.when(c < n_chunks // 2)
    def _():
        for v_cols, part_f, part_b in parts:
            oacc_ref[rows_f, v_cols] = part_f
            oacc_ref[rows_b, v_cols] = part_b

    @pl.when(c >= n_chunks // 2)
    def _():
        for v_cols, part_f, part_b in parts:
            out_ref[0, :, v_cols] = (oacc_ref[rows_f, v_cols] + part_f).astype(BF16)
            out_ref[1, :, v_cols] = (oacc_ref[rows_b, v_cols] + part_b).astype(BF16)


def _ret_branch(z, batch, seq, ret_decay):
    L = RET_CHUNK
    n_chunks = seq // L
    half = n_chunks // 2
    groups = RET_HEADS // RET_HPS
    assert n_chunks % 2 == 0 and RET_HEADS % RET_HPS == 0

    def fwd(bg, c):
        return (bg // groups) * n_chunks + c

    def bwd(bg, c):
        return (bg // groups) * n_chunks + (n_chunks - 1 - c)

    def zcol(col, width, bg):
        return col // width + bg % groups

    qk = (L, RET_HPS * RET_QK_DIM)
    vv = (L, RET_HPS * RET_V_DIM)
    in_specs = [pl.BlockSpec(memory_space=pltpu.SMEM)]
    for row in (fwd, bwd):
        in_specs += [
            pl.BlockSpec(qk, lambda bg, c, row=row: (row(bg, c), zcol(Z_RET_Q, qk[1], bg))),
            pl.BlockSpec(qk, lambda bg, c, row=row: (row(bg, c), zcol(Z_RET_K, qk[1], bg))),
            pl.BlockSpec(vv, lambda bg, c, row=row: (row(bg, c), zcol(Z_RET_V, vv[1], bg))),
        ]
    return pl.pallas_call(
        functools.partial(_ret_kernel, n_chunks),
        out_shape=jax.ShapeDtypeStruct((batch, half, 2, L, RET_HEADS * RET_V_DIM), BF16),
        grid=(batch * groups, n_chunks),
        in_specs=in_specs,
        out_specs=pl.BlockSpec((None, None, 2, L, vv[1]),
                               lambda bg, c: (bg // groups, jnp.maximum(c - half, 0), 0, 0, bg % groups)),
        scratch_shapes=[
            pltpu.VMEM((seq, vv[1]), F32),
            pltpu.VMEM((RET_HPS, 2, RET_QK_DIM, RET_V_DIM), F32),
            pltpu.VMEM((RET_HPS, L, L), F32),
            pltpu.VMEM((RET_HPS, 4, L, LANES), F32),
            pltpu.VMEM((RET_HPS, 2, RET_QK_DIM, LANES), F32),
        ],
        compiler_params=pltpu.CompilerParams(
            dimension_semantics=("arbitrary", "arbitrary"), vmem_limit_bytes=VMEM_LIMIT_BYTES),
        name="retention",
    )(ret_decay, z, z, z, z, z, z)


def _attn_kernel(seq_blocks, sink_ref, q_ref, kl_ref, km_ref, kr_ref, vl_ref, vm_ref, vr_ref,
                 gate_ref, out_ref):
    T = ATTN_BLOCK
    GW = ATTN_GROUP * HEAD_DIM
    blk = pl.program_id(0) % seq_blocks
    lane = lax.broadcasted_iota(jnp.int32, (T, GW), 1)
    head_masks = [jnp.where(lane // HEAD_DIM == h, 1.0, 0.0).astype(BF16) for h in range(ATTN_GROUP)]
    row = lax.broadcasted_iota(jnp.int32, (T, T), 0)
    col = lax.broadcasted_iota(jnp.int32, (T, T), 1)
    mask_l = (col >= row) & (blk > 0)
    mask_r = (col <= row) & (blk < seq_blocks - 1)
    lane_o = lax.broadcasted_iota(jnp.int32, (T, LANES), 1)

    def scores(g):
        lanes = slice(g * GW, (g + 1) * GW)
        qg = q_ref[:, lanes]
        q_heads = jnp.concatenate([qg * m for m in head_masks], axis=0)
        k_win = jnp.concatenate([kl_ref[:, lanes], km_ref[:, lanes], kr_ref[:, lanes]], axis=0)
        return lax.dot_general(q_heads, k_win, (((1,), (1,)), ((), ())), preferred_element_type=F32)

    def softmax(g, s):
        ps, invs = [], []
        for h in range(ATTN_GROUP):
            rows = slice(h * T, (h + 1) * T)
            sl = jnp.where(mask_l, s[rows, :T], -1e30)
            sm = s[rows, T:2 * T]
            sr = jnp.where(mask_r, s[rows, 2 * T:], -1e30)
            sink = sink_ref[g * ATTN_GROUP + h] * LOG2_E
            m = jnp.maximum(jnp.max(jnp.maximum(jnp.maximum(sl, sm), sr), axis=-1, keepdims=True), sink)
            el = jnp.exp2(sl - m)
            em = jnp.exp2(sm - m)
            er = jnp.exp2(sr - m)
            invs.append(1.0 / (jnp.sum(el + em + er, axis=-1, keepdims=True) + jnp.exp2(sink - m)))
            ps.append(jnp.concatenate([el, em, er], axis=1).astype(BF16))
        return ps, invs

    def values(g, ps, invs):
        lanes = slice(g * GW, (g + 1) * GW)
        outs = []
        v_win = jnp.concatenate([v_ref[:, g * LANES:(g + 1) * LANES] for v_ref in (vl_ref, vm_ref, vr_ref)],
                                axis=0)
        for hb in range(GW // LANES):
            o2 = jnp.dot(jnp.concatenate(ps[2 * hb:2 * hb + 2], axis=0), v_win, preferred_element_type=F32)
            outs.append(jnp.where(lane_o < HEAD_DIM, o2[:T] * invs[2 * hb], o2[T:] * invs[2 * hb + 1]))
        gate = gate_ref[:, lanes].astype(F32)
        out_ref[:, lanes] = (jnp.concatenate(outs, axis=1) * _silu(gate)).astype(BF16)

    ss = [scores(g) for g in range(ATTN_KV_HEADS)]
    ps = [softmax(g, ss[g]) for g in range(ATTN_KV_HEADS)]
    for g in range(ATTN_KV_HEADS):
        values(g, *ps[g])


def _attn_branch(z, qp, kp, vp, seq, attn_sink):
    n = z.shape[0]
    T = ATTN_BLOCK
    seq_blocks = seq // T
    last = n // T - 1
    blk = (T, D_MODEL)

    def here(col=0):
        return pl.BlockSpec(blk, lambda i: (i, col // D_MODEL))

    def window(width):
        return [pl.BlockSpec((T, width), lambda i: (jnp.maximum(i - 1, 0), 0)),
                pl.BlockSpec((T, width), lambda i: (i, 0)),
                pl.BlockSpec((T, width), lambda i: (jnp.minimum(i + 1, last), 0))]

    return pl.pallas_call(
        functools.partial(_attn_kernel, seq_blocks),
        out_shape=jax.ShapeDtypeStruct((n, D_MODEL), BF16),
        grid=(n // T,),
        in_specs=[pl.BlockSpec(memory_space=pltpu.SMEM), here()] + window(D_MODEL) + window(V_REP_WIDTH)
        + [here(Z_ATT_GATE)],
        out_specs=here(),
        compiler_params=pltpu.CompilerParams(
            dimension_semantics=("arbitrary",), vmem_limit_bytes=VMEM_LIMIT_BYTES),
        name="window_attn",
    )(attn_sink, qp, kp, kp, kp, vp, vp, vp, z)


def _out_kernel(x_ref, uc_ref, o_ref, rg0_ref, rg1_ref, ua_ref, g0_ref, g1_ref, g2_ref, bg_ref,
                wc_ref, wr_ref, wa_ref, wo_ref, out_ref, ur_ref):
    def gate(g_ref, i):
        return _sigmoid(g_ref[...].astype(F32) + bg_ref[:, i * D_MODEL:(i + 1) * D_MODEL])

    for r in range(OUT_TM // OUT_NORM_ROWS):
        rows = slice(r * OUT_NORM_ROWS, (r + 1) * OUT_NORM_ROWS)
        for h in range(RET_HEADS):
            cols = slice(h * RET_V_DIM, (h + 1) * RET_V_DIM)
            o = o_ref[rows, cols].astype(F32)
            mu = jnp.mean(o, axis=-1, keepdims=True)
            d = o - mu
            var = jnp.mean(d * d, axis=-1, keepdims=True)
            rg_ref = (rg0_ref, rg1_ref)[h * RET_V_DIM // D_MODEL]
            lo = h * RET_V_DIM % D_MODEL
            g = rg_ref[rows, lo:lo + RET_V_DIM].astype(F32)
            ur_ref[rows, cols] = (d * lax.rsqrt(var + EPS) * _silu(g)).astype(BF16)

    merged = gate(g0_ref, 0) * jnp.dot(uc_ref[...], wc_ref[...], preferred_element_type=F32)
    merged = merged + gate(g1_ref, 1) * jnp.dot(ur_ref[...], wr_ref[...], preferred_element_type=F32)
    merged = merged + gate(g2_ref, 2) * jnp.dot(ua_ref[...], wa_ref[...], preferred_element_type=F32)
    out_ref[...] = x_ref[...] + jnp.dot(merged.astype(BF16), wo_ref[...], preferred_element_type=F32)


def _out_proj(x2, u_conv, u_ret, u_attn, z, seq, b_gate, w_conv_out, w_ret_out, w_attn_out, w_out, layer):
    n = x2.shape[0]
    n_chunks = seq // OUT_TM
    half = n_chunks // 2

    def rows(width, col=0):
        return pl.BlockSpec((OUT_TM, width), lambda i: (i, col // width))

    def whole(shape):
        return pl.BlockSpec((None,) + shape, lambda i: (layer, 0, 0))

    def ret_index(i):
        b, c = i // n_chunks, i % n_chunks
        upper = c >= half
        return (b, jnp.where(upper, c - half, half - 1 - c), jnp.where(upper, 0, 1), 0, 0)

    return pl.pallas_call(
        _out_kernel,
        out_shape=jax.ShapeDtypeStruct((n, D_MODEL), F32),
        grid=(n // OUT_TM,),
        in_specs=[rows(D_MODEL), rows(D_MODEL),
                  pl.BlockSpec((None, None, None, OUT_TM, 2 * D_MODEL), ret_index),
                  rows(D_MODEL, Z_RET_GATE), rows(D_MODEL, Z_RET_GATE + D_MODEL),
                  rows(D_MODEL),
                  rows(D_MODEL, Z_MERGE), rows(D_MODEL, Z_MERGE + D_MODEL), rows(D_MODEL, Z_MERGE + 2 * D_MODEL),
                  whole((1, 3 * D_MODEL)),
                  whole((D_MODEL, D_MODEL)), whole((2 * D_MODEL, D_MODEL)),
                  whole((D_MODEL, D_MODEL)), whole((D_MODEL, D_MODEL))],
        out_specs=rows(D_MODEL),
        scratch_shapes=[pltpu.VMEM((OUT_TM, RET_HEADS * RET_V_DIM), BF16)],
        compiler_params=pltpu.CompilerParams(
            dimension_semantics=("arbitrary",), vmem_limit_bytes=VMEM_LIMIT_BYTES),
        name="out_proj",
    )(x2, u_conv, u_ret, z, z, u_attn, z, z, z, b_gate.reshape(-1, 1, 3 * D_MODEL),
      w_conv_out, w_ret_out, w_attn_out, w_out)


def _rotary_tables(seq):
    pos = jnp.arange(seq, dtype=jnp.int32).astype(F32)
    half_r = RET_QK_DIM // 2
    inv_r = RET_THETA ** (-jnp.arange(half_r, dtype=F32) / half_r)
    ang_r = pos[:, None] * inv_r[None, :]
    half_a = ROPE_DIM // 2
    inv_a = ROPE_THETA ** (-jnp.arange(half_a, dtype=F32) / half_a)
    ang_a = pos[:, None] * inv_a[None, :]
    ca, sa = jnp.cos(ang_a), jnp.sin(ang_a)
    rest = HEAD_DIM - ROPE_DIM
    cos64 = jnp.concatenate([ca, ca, jnp.ones((seq, rest), F32)], axis=1)
    sin64 = jnp.concatenate([-sa, sa, jnp.zeros((seq, rest), F32)], axis=1)
    rep = LANES // HEAD_DIM
    return jnp.cos(ang_r), jnp.sin(ang_r), jnp.tile(cos64, (1, rep)), jnp.tile(sin64, (1, rep))


def kernel(x, norm_g, w_in, b_gate, conv_dw, conv_b, conv_ln_g, conv_ln_b, ret_decay,
           q_norm_g, k_norm_g, attn_sink, w_conv_out, w_ret_out, w_attn_out, w_out):
    batch, seq, d = x.shape
    depth = norm_g.shape[0]
    assert d == D_MODEL and seq % max(CONV_TILE, 2 * RET_CHUNK, PREP_TILE, ATTN_BLOCK) == 0
    assert (batch * seq) % PROJ_TM == 0 and w_in.shape[-1] == Z_WIDTH
    cos_r, sin_r, cos_a, sin_a = _rotary_tables(seq)
    x2 = x.reshape(batch * seq, d)
    out_weights = [w.astype(BF16) for w in (w_conv_out, w_ret_out, w_attn_out, w_out)]
    for l in range(depth):
        z = _in_proj(x2, norm_g, w_in, cos_r, sin_r, l)
        qp, kp, vp = _prep(z, seq, cos_a, sin_a, q_norm_g[l], k_norm_g[l])
        u_conv = _conv_branch(z, seq, conv_dw[l], conv_b[l], conv_ln_g[l], conv_ln_b[l])
        u_ret = _ret_branch(z, batch, seq, ret_decay[l])
        u_attn = _attn_branch(z, qp, kp, vp, seq, attn_sink[l])
        x2 = _out_proj(x2, u_conv, u_ret, u_attn, z, seq, b_gate, *out_weights, l)
    return x2.reshape(batch, seq, d)
```

```python
import functools
import math

import jax
import jax.numpy as jnp
from jax import lax
from jax.experimental import pallas as pl
from jax.experimental.pallas import tpu as pltpu

F32 = jnp.float32
BF16 = jnp.bfloat16

D_MODEL = 1024
HEAD_DIM = 64
EPS = 1e-6
CONV_KERNEL = 31
CONV_HALF = CONV_KERNEL // 2
RET_HEADS = 4
RET_QK_DIM = 256
RET_V_DIM = 512
RET_THETA = 10000.0
ATTN_HEADS = 16
ATTN_KV_HEADS = 4
ATTN_GROUP = ATTN_HEADS // ATTN_KV_HEADS
ATTN_BLOCK = 128
ROPE_THETA = 500000.0
ROPE_DIM = HEAD_DIM // 4
KV_WIDTH = ATTN_KV_HEADS * HEAD_DIM
LOG2_E = math.log2(math.e)

LANES = 128
VMEM_LIMIT_BYTES = 56 * 1024 * 1024

Z_CONV_A, Z_CONV_B, Z_CONV_GATE = 0, 1024, 2048
Z_RET_Q, Z_RET_K, Z_RET_V, Z_RET_GATE = 3072, 4096, 5120, 7168
Z_ATT_Q, Z_ATT_GATE, Z_MERGE, Z_ATT_KV = 9216, 10240, 11264, 14336
Z_WIDTH = Z_ATT_KV + 2 * KV_WIDTH
REF_ATT_KV = Z_ATT_Q + D_MODEL

PROJ_TM, PROJ_TN, PROJ_ROT_ROWS = 2048, 512, 512
NORM_ROWS = 256
CONV_TILE, CONV_ROWS, CONV_HALO, CONV_TOKENS, CONV_NORM_UNROLL = 512, 32, 16, 32, 4
CBLK = D_MODEL // LANES
RET_CHUNK = 256
RET_HPS = 2
PREP_TILE, PREP_ROWS = 512, 128
V_REP_WIDTH = ATTN_KV_HEADS * LANES
OUT_TM = RET_CHUNK
OUT_NORM_ROWS = 64


def _sigmoid(x):
    return 1.0 / (1.0 + jnp.exp(-x))


def _silu(x):
    return x * _sigmoid(x)


def _in_proj_kernel(x_ref, g_ref, w_ref, cos_ref, sin_ref, z_ref, h_ref):
    j = pl.program_id(1)

    @pl.when(j == 0)
    def _():
        def norm_rows(r, carry):
            rows = pl.ds(pl.multiple_of(r * NORM_ROWS, NORM_ROWS), NORM_ROWS)
            xv = x_ref[rows, :]
            ms = jnp.mean(xv * xv, axis=-1, keepdims=True)
            h_ref[rows, :] = (xv * lax.rsqrt(ms + EPS) * g_ref[...]).astype(BF16)
            return carry

        lax.fori_loop(0, PROJ_TM // NORM_ROWS, norm_rows, 0)

    def product():
        return jnp.dot(h_ref[...], w_ref[...].astype(BF16), preferred_element_type=F32)

    rotary_tile = (j >= Z_RET_Q // PROJ_TN) & (j < Z_RET_V // PROJ_TN)

    @pl.when(jnp.logical_not(rotary_tile))
    def _():
        z_ref[...] = product().astype(BF16)

    @pl.when(rotary_tile)
    def _():
        scale = jnp.where(j >= Z_RET_K // PROJ_TN, RET_QK_DIM ** -0.5, 1.0)
        half = RET_QK_DIM // 2
        w = w_ref[...].astype(BF16)
        for r in range(PROJ_TM // PROJ_ROT_ROWS):
            rows = slice(r * PROJ_ROT_ROWS, (r + 1) * PROJ_ROT_ROWS)
            acc = jnp.dot(h_ref[rows, :], w, preferred_element_type=F32)
            c = cos_ref[rows, :] * scale
            s = sin_ref[rows, :] * scale
            for hd in range(PROJ_TN // RET_QK_DIM):
                lo = slice(hd * RET_QK_DIM, hd * RET_QK_DIM + half)
                hi = slice(hd * RET_QK_DIM + half, (hd + 1) * RET_QK_DIM)
                z_ref[rows, lo] = (acc[:, lo] * c - acc[:, hi] * s).astype(BF16)
                z_ref[rows, hi] = (acc[:, hi] * c + acc[:, lo] * s).astype(BF16)


def _in_proj(x2, norm_g, w_in, cos_r, sin_r, layer):
    n = x2.shape[0]
    depth = norm_g.shape[0]
    seq_tiles = cos_r.shape[0] // PROJ_TM
    table = pl.BlockSpec((PROJ_TM, LANES), lambda i, j: (i % seq_tiles, 0))
    kv_tile = REF_ATT_KV // PROJ_TN
    n_tiles = Z_WIDTH // PROJ_TN
    assert REF_ATT_KV % PROJ_TN == 0 and 2 * KV_WIDTH == PROJ_TN

    def w_tile(j):
        return jnp.where(j == n_tiles - 1, kv_tile, jnp.where(j >= kv_tile, j + 1, j))

    return pl.pallas_call(
        _in_proj_kernel,
        out_shape=jax.ShapeDtypeStruct((n, Z_WIDTH), BF16),
        grid=(n // PROJ_TM, n_tiles),
        in_specs=[
            pl.BlockSpec((PROJ_TM, D_MODEL), lambda i, j: (i, 0)),
            pl.BlockSpec((None, 1, D_MODEL), lambda i, j: (layer, 0, 0)),
            pl.BlockSpec((None, D_MODEL, PROJ_TN), lambda i, j: (layer, 0, w_tile(j))),
            table, table,
        ],
        out_specs=pl.BlockSpec((PROJ_TM, PROJ_TN), lambda i, j: (i, j)),
        scratch_shapes=[pltpu.VMEM((PROJ_TM, D_MODEL), BF16)],
        compiler_params=pltpu.CompilerParams(
            dimension_semantics=("arbitrary", "arbitrary"), vmem_limit_bytes=VMEM_LIMIT_BYTES),
        name="in_proj",
    )(x2, norm_g.reshape(depth, 1, D_MODEL), w_in, cos_r, sin_r)


def _prep_kernel(aq_ref, akv_ref, cosa_ref, sina_ref, gq_ref, gk_ref, avg_ref, swap_ref,
                 repk_ref, repv_ref, qp_ref, kp_ref, vp_ref, k_ref):
    avg = avg_ref[...]
    swap = swap_ref[...]
    q_scale = LOG2_E * HEAD_DIM ** -0.5

    def chunk(r, carry):
        rows = pl.ds(pl.multiple_of(r * PREP_ROWS, PREP_ROWS), PREP_ROWS)
        cos_a = cosa_ref[rows, :]
        sin_a = sina_ref[rows, :]

        def norm_rope(x, g_ref, scale):
            ms = jnp.dot((x * x).astype(BF16), avg, preferred_element_type=F32)
            xg = x * g_ref[...]
            partner = jnp.dot(xg.astype(BF16), swap, preferred_element_type=F32)
            return (xg * cos_a + partner * sin_a) * (lax.rsqrt(ms + EPS) * scale)

        for c in range(D_MODEL // LANES):
            lanes = slice(c * LANES, (c + 1) * LANES)
            qp_ref[rows, lanes] = norm_rope(aq_ref[rows, lanes].astype(F32), gq_ref, q_scale).astype(BF16)
        for c in range(KV_WIDTH // LANES):
            lanes = slice(c * LANES, (c + 1) * LANES)
            k_ref[rows, lanes] = norm_rope(akv_ref[rows, lanes].astype(F32), gk_ref, 1.0).astype(BF16)
        return carry

    lax.fori_loop(0, PREP_TILE // PREP_ROWS, chunk, 0)
    kp_ref[...] = jnp.dot(k_ref[...], repk_ref[...], preferred_element_type=F32).astype(BF16)
    vp_ref[...] = jnp.dot(akv_ref[:, KV_WIDTH:], repv_ref[...], preferred_element_type=F32).astype(BF16)


def _prep(z, seq, cos_a, sin_a, q_norm_g, k_norm_g):
    n = z.shape[0]
    seq_tiles = seq // PREP_TILE
    heads_per_block = LANES // HEAD_DIM
    gq = jnp.tile(q_norm_g, heads_per_block).reshape(1, LANES)
    gk = jnp.tile(k_norm_g, heads_per_block).reshape(1, LANES)
    lane = jnp.arange(LANES)
    head = lane // HEAD_DIM
    avg = jnp.where(head[:, None] == head[None, :], 1.0 / HEAD_DIM, 0.0).astype(BF16)
    within = lane % HEAD_DIM
    half = ROPE_DIM // 2
    partner = jnp.where(within < half, lane + half, jnp.where(within < ROPE_DIM, lane - half, -1))
    swap = (lane[:, None] == partner[None, :]).astype(BF16)
    src = jnp.arange(KV_WIDTH)

    def rep(slots):
        dst = jnp.arange(ATTN_KV_HEADS * slots * HEAD_DIM)
        return ((src[:, None] // HEAD_DIM == dst[None, :] // (slots * HEAD_DIM))
                & (src[:, None] % HEAD_DIM == dst[None, :] % HEAD_DIM)).astype(BF16)

    def rows(width, col):
        return pl.BlockSpec((PREP_TILE, width), lambda i: (i, col // width))

    def whole(shape):
        return pl.BlockSpec(shape, lambda i: (0, 0))

    tab = pl.BlockSpec((PREP_TILE, LANES), lambda i: (i % seq_tiles, 0))
    wide = jax.ShapeDtypeStruct((n, D_MODEL), BF16)
    wide_spec = pl.BlockSpec((PREP_TILE, D_MODEL), lambda i: (i, 0))
    return pl.pallas_call(
        _prep_kernel,
        out_shape=(wide, wide, jax.ShapeDtypeStruct((n, V_REP_WIDTH), BF16)),
        grid=(n // PREP_TILE,),
        in_specs=[rows(D_MODEL, Z_ATT_Q), rows(2 * KV_WIDTH, Z_ATT_KV),
                  tab, tab, whole((1, LANES)), whole((1, LANES)),
                  whole((LANES, LANES)), whole((LANES, LANES)),
                  whole((KV_WIDTH, D_MODEL)), whole((KV_WIDTH, V_REP_WIDTH))],
        out_specs=(wide_spec, wide_spec, pl.BlockSpec((PREP_TILE, V_REP_WIDTH), lambda i: (i, 0))),
        scratch_shapes=[pltpu.VMEM((PREP_TILE, KV_WIDTH), BF16)],
        compiler_params=pltpu.CompilerParams(
            dimension_semantics=("arbitrary",), vmem_limit_bytes=VMEM_LIMIT_BYTES),
        name="qk_prep",
    )(z, z, cos_a, sin_a, gq, gk, avg, swap, rep(ATTN_GROUP), rep(LANES // HEAD_DIM))


def _conv_kernel(seq_tiles, a_ref, b_ref, gate_ref, ap_ref, bp_ref, an_ref, bn_ref,
                 w_ref, cb_ref, lg_ref, lb_ref, out_ref, v_ref, y_ref):
    t = pl.program_id(0) % seq_tiles

    def glu(a, b):
        return a.astype(F32) * _sigmoid(b.astype(F32))

    def scatter(val, tok0):
        for c in range(CBLK):
            v_ref[pl.ds(tok0 * CBLK + c, val.shape[0], stride=CBLK), :] = val[:, c * LANES:(c + 1) * LANES]

    scatter(jnp.where(t == 0, 0.0, glu(ap_ref[...], bp_ref[...])), 0)
    scatter(jnp.where(t == seq_tiles - 1, 0.0, glu(an_ref[...], bn_ref[...])), CONV_HALO + CONV_TILE)

    def fill(r, carry):
        for u in range(CONV_NORM_UNROLL):
            r0 = pl.multiple_of((r * CONV_NORM_UNROLL + u) * CONV_ROWS, CONV_ROWS)
            rows = pl.ds(r0, CONV_ROWS)
            scatter(glu(a_ref[rows, :], b_ref[rows, :]), r0 + CONV_HALO)
        return carry

    lax.fori_loop(0, CONV_TILE // (CONV_ROWS * CONV_NORM_UNROLL), fill, 0)

    bias = cb_ref[...]

    def conv_tokens(r, carry):
        t0 = r * CONV_TOKENS
        acc = [bias] * CONV_TOKENS
        for k in range(CONV_KERNEL):
            wk = w_ref[k * CBLK:(k + 1) * CBLK, :]
            for i in range(CONV_TOKENS):
                src = pl.multiple_of((t0 + (i + k - CONV_HALF + CONV_HALO)) * CBLK, CBLK)
                acc[i] = acc[i] + v_ref[pl.ds(src, CBLK), :] * wk
        for i in range(CONV_TOKENS):
            y_ref[pl.ds(pl.multiple_of((t0 + i) * CBLK, CBLK), CBLK), :] = acc[i]
        return carry

    lax.fori_loop(0, CONV_TILE // CONV_TOKENS, conv_tokens, 0)

    def norm_rows(r, carry):
        for u in range(CONV_NORM_UNROLL):
            r0 = pl.multiple_of((r * CONV_NORM_UNROLL + u) * CONV_ROWS, CONV_ROWS)
            y = jnp.concatenate(
                [y_ref[pl.ds(r0 * CBLK + c, CONV_ROWS, stride=CBLK), :] for c in range(CBLK)], axis=1)
            mu = jnp.mean(y, axis=-1, keepdims=True)
            d = y - mu
            var = jnp.mean(d * d, axis=-1, keepdims=True)
            yn = d * lax.rsqrt(var + EPS) * lg_ref[...] + lb_ref[...]
            g = gate_ref[pl.ds(r0, CONV_ROWS), :].astype(F32)
            out_ref[pl.ds(r0, CONV_ROWS), :] = (_silu(yn) * _silu(g)).astype(BF16)
        return carry

    lax.fori_loop(0, CONV_TILE // (CONV_ROWS * CONV_NORM_UNROLL), norm_rows, 0)


def _conv_branch(z, seq, conv_dw, conv_b, ln_g, ln_b):
    n = z.shape[0]
    seq_tiles = seq // CONV_TILE
    halo_per_tile = CONV_TILE // CONV_HALO
    last_halo = n // CONV_HALO - 1
    w_tiles = conv_dw.reshape(CONV_KERNEL * CBLK, LANES)

    def main(col):
        return pl.BlockSpec((CONV_TILE, D_MODEL), lambda i: (i, col // D_MODEL))

    def prev(col):
        return pl.BlockSpec((CONV_HALO, D_MODEL),
                            lambda i: (jnp.maximum(i * halo_per_tile - 1, 0), col // D_MODEL))

    def nxt(col):
        return pl.BlockSpec((CONV_HALO, D_MODEL),
                            lambda i: (jnp.minimum((i + 1) * halo_per_tile, last_halo), col // D_MODEL))

    vec = pl.BlockSpec((1, D_MODEL), lambda i: (0, 0))
    return pl.pallas_call(
        functools.partial(_conv_kernel, seq_tiles),
        out_shape=jax.ShapeDtypeStruct((n, D_MODEL), BF16),
        grid=(n // CONV_TILE,),
        in_specs=[main(Z_CONV_A), main(Z_CONV_B), main(Z_CONV_GATE),
                  prev(Z_CONV_A), prev(Z_CONV_B), nxt(Z_CONV_A), nxt(Z_CONV_B),
                  pl.BlockSpec((CONV_KERNEL * CBLK, LANES), lambda i: (0, 0)),
                  pl.BlockSpec((CBLK, LANES), lambda i: (0, 0)), vec, vec],
        out_specs=pl.BlockSpec((CONV_TILE, D_MODEL), lambda i: (i, 0)),
        scratch_shapes=[pltpu.VMEM(((CONV_TILE + 2 * CONV_HALO) * CBLK, LANES), F32),
                        pltpu.VMEM((CONV_TILE * CBLK, LANES), F32)],
        compiler_params=pltpu.CompilerParams(
            dimension_semantics=("arbitrary",), vmem_limit_bytes=VMEM_LIMIT_BYTES),
        name="conv_branch",
    )(z, z, z, z, z, z, z, w_tiles, conv_b.reshape(CBLK, LANES), ln_g.reshape(1, D_MODEL),
      ln_b.reshape(1, D_MODEL))


def _ret_kernel(n_chunks, rd_ref, qf_ref, kf_ref, vf_ref, qb_ref, kb_ref, vb_ref,
                out_ref, oacc_ref, state_ref, dmat_ref, dec_ref, cdec_ref):
    L = RET_CHUNK
    head0 = (pl.program_id(0) % (RET_HEADS // RET_HPS)) * RET_HPS
    c = pl.program_id(1)

    @pl.when(c == 0)
    def _():
        idx = lax.broadcasted_iota(jnp.int32, (L, LANES), 0).astype(F32)
        i = lax.broadcasted_iota(jnp.int32, (L, L), 0)
        j = lax.broadcasted_iota(jnp.int32, (L, L), 1)
        d = (i - j).astype(F32)
        state_ref[...] = jnp.zeros_like(state_ref)
        for hd in range(RET_HPS):
            lg_f = -jnp.exp(jnp.full((L, LANES), rd_ref[0, head0 + hd], F32))
            lg_b = -jnp.exp(jnp.full((L, LANES), rd_ref[1, head0 + hd], F32))
            dec_ref[hd, 0] = jnp.exp(lg_f * (idx + 1.0))
            dec_ref[hd, 1] = jnp.exp(lg_f * (L - 1.0 - idx))
            dec_ref[hd, 2] = jnp.exp(lg_b * (L - idx))
            dec_ref[hd, 3] = jnp.exp(lg_b * idx)
            cdec_ref[hd, 0] = jnp.exp(lg_f[:RET_QK_DIM] * float(L))
            cdec_ref[hd, 1] = jnp.exp(lg_b[:RET_QK_DIM] * float(L))
            lgm_f = -jnp.exp(jnp.full((L, L), rd_ref[0, head0 + hd], F32))
            lgm_b = -jnp.exp(jnp.full((L, L), rd_ref[1, head0 + hd], F32))
            dmat_ref[hd] = jnp.where(i >= j, jnp.exp(lgm_f * jnp.maximum(d, 0.0)),
                                     jnp.exp(lgm_b * jnp.maximum(-d, 0.0)))

    rows_f = pl.ds(pl.multiple_of(c * L, L), L)
    rows_b = pl.ds(pl.multiple_of((n_chunks - 1 - c) * L, L), L)
    parts = []
    for hd in range(RET_HPS):
        qk_cols = slice(hd * RET_QK_DIM, (hd + 1) * RET_QK_DIM)
        v_cols = slice(hd * RET_V_DIM, (hd + 1) * RET_V_DIM)

        def decayed(ref, dec):
            return (ref[:, qk_cols].astype(F32)
                    * jnp.concatenate([dec] * (RET_QK_DIM // LANES), axis=1)).astype(BF16)

        def advance(direction, q_ref, k_ref, v_ref):
            state = state_ref[hd, direction]
            cross = jnp.dot(decayed(q_ref, dec_ref[hd, 2 * direction]), state.astype(BF16),
                            preferred_element_type=F32)
            kv = lax.dot_general(decayed(k_ref, dec_ref[hd, 2 * direction + 1]), v_ref[:, v_cols],
                                 (((0,), (0,)), ((), ())), preferred_element_type=F32)
            chunk_dec = jnp.concatenate([cdec_ref[hd, direction]] * (RET_V_DIM // LANES), axis=1)
            state_ref[hd, direction] = chunk_dec * state + kv
            return cross

        s = lax.dot_general(qf_ref[:, qk_cols], kf_ref[:, qk_cols], (((1,), (1,)), ((), ())),
                            preferred_element_type=F32)
        part_f = jnp.dot((s * dmat_ref[hd]).astype(BF16), vf_ref[:, v_cols], preferred_element_type=F32)
        part_f = part_f + advance(0, qf_ref, kf_ref, vf_ref)
        part_b = advance(1, qb_ref, kb_ref, vb_ref)
        parts.append((v_cols, part_f, part_b))

    @pl.when(c < n_chunks // 2)
    def _():
        for v_cols, part_f, part_b in parts:
            oacc_ref[rows_f, v_cols] = part_f
            oacc_ref[rows_b, v_cols] = part_b

    @pl.when(c >= n_chunks // 2)
    def _():
        for v_cols, part_f, part_b in parts:
            out_ref[0, :, v_cols] = (oacc_ref[rows_f, v_cols] + part_f).astype(BF16)
            out_ref[1, :, v_cols] = (oacc_ref[rows_b, v_cols] + part_b).astype(BF16)


def _ret_branch(z, batch, seq, ret_decay):
    L = RET_CHUNK
    n_chunks = seq // L
    half = n_chunks // 2
    groups = RET_HEADS // RET_HPS
    assert n_chunks % 2 == 0 and RET_HEADS % RET_HPS == 0

    def fwd(bg, c):
        return (bg // groups) * n_chunks + c

    def bwd(bg, c):
        return (bg // groups) * n_chunks + (n_chunks - 1 - c)

    def zcol(col, width, bg):
        return col // width + bg % groups

    qk = (L, RET_HPS * RET_QK_DIM)
    vv = (L, RET_HPS * RET_V_DIM)
    in_specs = [pl.BlockSpec(memory_space=pltpu.SMEM)]
    for row in (fwd, bwd):
        in_specs += [
            pl.BlockSpec(qk, lambda bg, c, row=row: (row(bg, c), zcol(Z_RET_Q, qk[1], bg))),
            pl.BlockSpec(qk, lambda bg, c, row=row: (row(bg, c), zcol(Z_RET_K, qk[1], bg))),
            pl.BlockSpec(vv, lambda bg, c, row=row: (row(bg, c), zcol(Z_RET_V, vv[1], bg))),
        ]
    return pl.pallas_call(
        functools.partial(_ret_kernel, n_chunks),
        out_shape=jax.ShapeDtypeStruct((batch, half, 2, L, RET_HEADS * RET_V_DIM), BF16),
        grid=(batch * groups, n_chunks),
        in_specs=in_specs,
        out_specs=pl.BlockSpec((None, None, 2, L, vv[1]),
                               lambda bg, c: (bg // groups, jnp.maximum(c - half, 0), 0, 0, bg % groups)),
        scratch_shapes=[
            pltpu.VMEM((seq, vv[1]), F32),
            pltpu.VMEM((RET_HPS, 2, RET_QK_DIM, RET_V_DIM), F32),
            pltpu.VMEM((RET_HPS, L, L), F32),
            pltpu.VMEM((RET_HPS, 4, L, LANES), F32),
            pltpu.VMEM((RET_HPS, 2, RET_QK_DIM, LANES), F32),
        ],
        compiler_params=pltpu.CompilerParams(
            dimension_semantics=("arbitrary", "arbitrary"), vmem_limit_bytes=VMEM_LIMIT_BYTES),
        name="retention",
    )(ret_decay, z, z, z, z, z, z)


def _attn_kernel(seq_blocks, sink_ref, q_ref, kl_ref, km_ref, kr_ref, vl_ref, vm_ref, vr_ref,
                 gate_ref, out_ref):
    T = ATTN_BLOCK
    GW = ATTN_GROUP * HEAD_DIM
    blk = pl.program_id(0) % seq_blocks
    lane = lax.broadcasted_iota(jnp.int32, (T, GW), 1)
    head_masks = [lane // HEAD_DIM == h for h in range(ATTN_GROUP)]
    row = lax.broadcasted_iota(jnp.int32, (T, T), 0)
    col = lax.broadcasted_iota(jnp.int32, (T, T), 1)
    mask_l = (col >= row) & (blk > 0)
    mask_r = (col <= row) & (blk < seq_blocks - 1)
    lane_o = lax.broadcasted_iota(jnp.int32, (T, LANES), 1)

    def scores(g):
        lanes = slice(g * GW, (g + 1) * GW)
        qg = q_ref[:, lanes]
        q_heads = jnp.concatenate([jnp.where(m, qg, jnp.zeros_like(qg)) for m in head_masks], axis=0)
        k_win = jnp.concatenate([kl_ref[:, lanes], km_ref[:, lanes], kr_ref[:, lanes]], axis=0)
        return lax.dot_general(q_heads, k_win, (((1,), (1,)), ((), ())), preferred_element_type=F32)

    def softmax(g, s):
        ps, invs = [], []
        for h in range(ATTN_GROUP):
            rows = slice(h * T, (h + 1) * T)
            sl = jnp.where(mask_l, s[rows, :T], -1e30)
            sm = s[rows, T:2 * T]
            sr = jnp.where(mask_r, s[rows, 2 * T:], -1e30)
            sink = sink_ref[g * ATTN_GROUP + h] * LOG2_E
            m = jnp.maximum(jnp.max(jnp.maximum(jnp.maximum(sl, sm), sr), axis=-1, keepdims=True), sink)
            el = jnp.exp2(sl - m)
            em = jnp.exp2(sm - m)
            er = jnp.exp2(sr - m)
            invs.append(1.0 / (jnp.sum(el + em + er, axis=-1, keepdims=True) + jnp.exp2(sink - m)))
            ps.append(jnp.concatenate([el, em, er], axis=1).astype(BF16))
        return ps, invs

    def values(g, ps, invs):
        lanes = slice(g * GW, (g + 1) * GW)
        outs = []
        v_win = jnp.concatenate([v_ref[:, g * LANES:(g + 1) * LANES] for v_ref in (vl_ref, vm_ref, vr_ref)],
                                axis=0)
        for hb in range(GW // LANES):
            o2 = jnp.dot(jnp.concatenate(ps[2 * hb:2 * hb + 2], axis=0), v_win, preferred_element_type=F32)
            outs.append(jnp.where(lane_o < HEAD_DIM, o2[:T] * invs[2 * hb], o2[T:] * invs[2 * hb + 1]))
        gate = gate_ref[:, lanes].astype(F32)
        out_ref[:, lanes] = (jnp.concatenate(outs, axis=1) * _silu(gate)).astype(BF16)

    ss = [scores(g) for g in range(ATTN_KV_HEADS)]
    ps = [softmax(g, ss[g]) for g in range(ATTN_KV_HEADS)]
    for g in range(ATTN_KV_HEADS):
        values(g, *ps[g])


def _attn_branch(z, qp, kp, vp, seq, attn_sink):
    n = z.shape[0]
    T = ATTN_BLOCK
    seq_blocks = seq // T
    last = n // T - 1
    blk = (T, D_MODEL)

    def here(col=0):
        return pl.BlockSpec(blk, lambda i: (i, col // D_MODEL))

    def window(width):
        return [pl.BlockSpec((T, width), lambda i: (jnp.maximum(i - 1, 0), 0)),
                pl.BlockSpec((T, width), lambda i: (i, 0)),
                pl.BlockSpec((T, width), lambda i: (jnp.minimum(i + 1, last), 0))]

    return pl.pallas_call(
        functools.partial(_attn_kernel, seq_blocks),
        out_shape=jax.ShapeDtypeStruct((n, D_MODEL), BF16),
        grid=(n // T,),
        in_specs=[pl.BlockSpec(memory_space=pltpu.SMEM), here()] + window(D_MODEL) + window(V_REP_WIDTH)
        + [here(Z_ATT_GATE)],
        out_specs=here(),
        compiler_params=pltpu.CompilerParams(
            dimension_semantics=("arbitrary",), vmem_limit_bytes=VMEM_LIMIT_BYTES),
        name="window_attn",
    )(attn_sink, qp, kp, kp, kp, vp, vp, vp, z)


def _out_kernel(x_ref, uc_ref, o_ref, rg0_ref, rg1_ref, ua_ref, g0_ref, g1_ref, g2_ref, bg_ref,
                wc_ref, wr_ref, wa_ref, wo_ref, out_ref, ur_ref):
    def gate(g_ref, i):
        return _sigmoid(g_ref[...].astype(F32) + bg_ref[:, i * D_MODEL:(i + 1) * D_MODEL])

    for r in range(OUT_TM // OUT_NORM_ROWS):
        rows = slice(r * OUT_NORM_ROWS, (r + 1) * OUT_NORM_ROWS)
        for h in range(RET_HEADS):
            cols = slice(h * RET_V_DIM, (h + 1) * RET_V_DIM)
            o = o_ref[rows, cols].astype(F32)
            mu = jnp.mean(o, axis=-1, keepdims=True)
            d = o - mu
            var = jnp.mean(d * d, axis=-1, keepdims=True)
            rg_ref = (rg0_ref, rg1_ref)[h * RET_V_DIM // D_MODEL]
            lo = h * RET_V_DIM % D_MODEL
            g = rg_ref[rows, lo:lo + RET_V_DIM].astype(F32)
            ur_ref[rows, cols] = (d * lax.rsqrt(var + EPS) * _silu(g)).astype(BF16)

    merged = gate(g0_ref, 0) * jnp.dot(uc_ref[...], wc_ref[...], preferred_element_type=F32)
    merged = merged + gate(g1_ref, 1) * jnp.dot(ur_ref[...], wr_ref[...], preferred_element_type=F32)
    merged = merged + gate(g2_ref, 2) * jnp.dot(ua_ref[...], wa_ref[...], preferred_element_type=F32)
    out_ref[...] = x_ref[...] + jnp.dot(merged.astype(BF16), wo_ref[...], preferred_element_type=F32)


def _out_proj(x2, u_conv, u_ret, u_attn, z, seq, b_gate, w_conv_out, w_ret_out, w_attn_out, w_out, layer):
    n = x2.shape[0]
    n_chunks = seq // OUT_TM
    half = n_chunks // 2

    def rows(width, col=0):
        return pl.BlockSpec((OUT_TM, width), lambda i: (i, col // width))

    def whole(shape):
        return pl.BlockSpec((None,) + shape, lambda i: (layer, 0, 0))

    def ret_index(i):
        b, c = i // n_chunks, i % n_chunks
        upper = c >= half
        return (b, jnp.where(upper, c - half, half - 1 - c), jnp.where(upper, 0, 1), 0, 0)

    return pl.pallas_call(
        _out_kernel,
        out_shape=jax.ShapeDtypeStruct((n, D_MODEL), F32),
        grid=(n // OUT_TM,),
        in_specs=[rows(D_MODEL), rows(D_MODEL),
                  pl.BlockSpec((None, None, None, OUT_TM, 2 * D_MODEL), ret_index),
                  rows(D_MODEL, Z_RET_GATE), rows(D_MODEL, Z_RET_GATE + D_MODEL),
                  rows(D_MODEL),
                  rows(D_MODEL, Z_MERGE), rows(D_MODEL, Z_MERGE + D_MODEL), rows(D_MODEL, Z_MERGE + 2 * D_MODEL),
                  whole((1, 3 * D_MODEL)),
                  whole((D_MODEL, D_MODEL)), whole((2 * D_MODEL, D_MODEL)),
                  whole((D_MODEL, D_MODEL)), whole((D_MODEL, D_MODEL))],
        out_specs=rows(D_MODEL),
        scratch_shapes=[pltpu.VMEM((OUT_TM, RET_HEADS * RET_V_DIM), BF16)],
        compiler_params=pltpu.CompilerParams(
            dimension_semantics=("arbitrary",), vmem_limit_bytes=VMEM_LIMIT_BYTES),
        name="out_proj",
    )(x2, u_conv, u_ret, z, z, u_attn, z, z, z, b_gate.reshape(-1, 1, 3 * D_MODEL),
      w_conv_out, w_ret_out, w_attn_out, w_out)


def _rotary_tables(seq):
    pos = jnp.arange(seq, dtype=jnp.int32).astype(F32)
    half_r = RET_QK_DIM // 2
    inv_r = RET_THETA ** (-jnp.arange(half_r, dtype=F32) / half_r)
    ang_r = pos[:, None] * inv_r[None, :]
    half_a = ROPE_DIM // 2
    inv_a = ROPE_THETA ** (-jnp.arange(half_a, dtype=F32) / half_a)
    ang_a = pos[:, None] * inv_a[None, :]
    ca, sa = jnp.cos(ang_a), jnp.sin(ang_a)
    rest = HEAD_DIM - ROPE_DIM
    cos64 = jnp.concatenate([ca, ca, jnp.ones((seq, rest), F32)], axis=1)
    sin64 = jnp.concatenate([-sa, sa, jnp.zeros((seq, rest), F32)], axis=1)
    rep = LANES // HEAD_DIM
    return jnp.cos(ang_r), jnp.sin(ang_r), jnp.tile(cos64, (1, rep)), jnp.tile(sin64, (1, rep))


def kernel(x, norm_g, w_in, b_gate, conv_dw, conv_b, conv_ln_g, conv_ln_b, ret_decay,
           q_norm_g, k_norm_g, attn_sink, w_conv_out, w_ret_out, w_attn_out, w_out):
    batch, seq, d = x.shape
    depth = norm_g.shape[0]
    assert d == D_MODEL and seq % max(CONV_TILE, 2 * RET_CHUNK, PREP_TILE, ATTN_BLOCK) == 0
    assert (batch * seq) % PROJ_TM == 0 and w_in.shape[-1] == Z_WIDTH
    cos_r, sin_r, cos_a, sin_a = _rotary_tables(seq)
    x2 = x.reshape(batch * seq, d)
    out_weights = [w.astype(BF16) for w in (w_conv_out, w_ret_out, w_attn_out, w_out)]
    for l in range(depth):
        z = _in_proj(x2, norm_g, w_in, cos_r, sin_r, l)
        qp, kp, vp = _prep(z, seq, cos_a, sin_a, q_norm_g[l], k_norm_g[l])
        u_conv = _conv_branch(z, seq, conv_dw[l], conv_b[l], conv_ln_g[l], conv_ln_b[l])
        u_ret = _ret_branch(z, batch, seq, ret_decay[l])
        u_attn = _attn_branch(z, qp, kp, vp, seq, attn_sink[l])
        x2 = _out_proj(x2, u_conv, u_ret, u_attn, z, seq, b_gate, *out_weights, l)
    return x2.reshape(batch, seq, d)
```

```python
import functools
import math

import jax
import jax.numpy as jnp
from jax import lax
from jax.experimental import pallas as pl
from jax.experimental.pallas import tpu as pltpu

F32 = jnp.float32
BF16 = jnp.bfloat16

D_MODEL = 1024
HEAD_DIM = 64
EPS = 1e-6
CONV_KERNEL = 31
CONV_HALF = CONV_KERNEL // 2
RET_HEADS = 4
RET_QK_DIM = 256
RET_V_DIM = 512
RET_THETA = 10000.0
ATTN_HEADS = 16
ATTN_KV_HEADS = 4
ATTN_GROUP = ATTN_HEADS // ATTN_KV_HEADS
ATTN_BLOCK = 128
ROPE_THETA = 500000.0
ROPE_DIM = HEAD_DIM // 4
KV_WIDTH = ATTN_KV_HEADS * HEAD_DIM
LOG2_E = math.log2(math.e)

LANES = 128
VMEM_LIMIT_BYTES = 56 * 1024 * 1024

Z_CONV_A, Z_CONV_B, Z_CONV_GATE = 0, 1024, 2048
Z_RET_Q, Z_RET_K, Z_RET_V, Z_RET_GATE = 3072, 4096, 5120, 7168
Z_ATT_Q, Z_ATT_GATE, Z_MERGE, Z_ATT_KV = 9216, 10240, 11264, 14336
Z_WIDTH = Z_ATT_KV + 2 * KV_WIDTH
REF_ATT_KV = Z_ATT_Q + D_MODEL

PROJ_TM, PROJ_TN, PROJ_ROT_ROWS = 2048, 512, 256
NORM_ROWS = 256
CONV_TILE, CONV_ROWS, CONV_HALO, CONV_TOKENS, CONV_NORM_UNROLL = 512, 32, 16, 32, 4
CBLK = D_MODEL // LANES
RET_CHUNK = 256
RET_HPS = 2
PREP_TILE, PREP_ROWS = 512, 128
V_REP_WIDTH = ATTN_KV_HEADS * LANES
OUT_TM = RET_CHUNK
OUT_NORM_ROWS = 64


def _sigmoid(x):
    return 1.0 / (1.0 + jnp.exp(-x))


def _silu(x):
    return x * _sigmoid(x)


def _in_proj_kernel(x_ref, g_ref, w_ref, cos_ref, sin_ref, bg_ref, z_ref, h_ref):
    j = pl.program_id(1)

    @pl.when(j == 0)
    def _():
        def norm_rows(r, carry):
            rows = pl.ds(pl.multiple_of(r * NORM_ROWS, NORM_ROWS), NORM_ROWS)
            xv = x_ref[rows, :]
            ms = jnp.mean(xv * xv, axis=-1, keepdims=True)
            h_ref[rows, :] = (xv * lax.rsqrt(ms + EPS) * g_ref[...]).astype(BF16)
            return carry

        lax.fori_loop(0, PROJ_TM // NORM_ROWS, norm_rows, 0)

    def product():
        return jnp.dot(h_ref[...], w_ref[...].astype(BF16), preferred_element_type=F32)

    rotary_tile = (j >= Z_RET_Q // PROJ_TN) & (j < Z_RET_V // PROJ_TN)
    merge_tile = (j >= Z_MERGE // PROJ_TN) & (j < Z_ATT_KV // PROJ_TN)

    @pl.when(jnp.logical_not(rotary_tile | merge_tile))
    def _():
        z_ref[...] = product().astype(BF16)

    @pl.when(merge_tile)
    def _():
        w = w_ref[...].astype(BF16)
        for r in range(PROJ_TM // PROJ_ROT_ROWS):
            rows = slice(r * PROJ_ROT_ROWS, (r + 1) * PROJ_ROT_ROWS)
            acc = jnp.dot(h_ref[rows, :], w, preferred_element_type=F32)
            z_ref[rows, :] = _sigmoid(acc + bg_ref[...]).astype(BF16)

    @pl.when(rotary_tile)
    def _():
        scale = jnp.where(j >= Z_RET_K // PROJ_TN, RET_QK_DIM ** -0.5, 1.0)
        half = RET_QK_DIM // 2
        w = w_ref[...].astype(BF16)
        for r in range(PROJ_TM // PROJ_ROT_ROWS):
            rows = slice(r * PROJ_ROT_ROWS, (r + 1) * PROJ_ROT_ROWS)
            acc = jnp.dot(h_ref[rows, :], w, preferred_element_type=F32)
            c = cos_ref[rows, :] * scale
            s = sin_ref[rows, :] * scale
            for hd in range(PROJ_TN // RET_QK_DIM):
                lo = slice(hd * RET_QK_DIM, hd * RET_QK_DIM + half)
                hi = slice(hd * RET_QK_DIM + half, (hd + 1) * RET_QK_DIM)
                z_ref[rows, lo] = (acc[:, lo] * c - acc[:, hi] * s).astype(BF16)
                z_ref[rows, hi] = (acc[:, hi] * c + acc[:, lo] * s).astype(BF16)


def _in_proj(x2, norm_g, w_in, b_gate, cos_r, sin_r, layer):
    n = x2.shape[0]
    depth = norm_g.shape[0]
    seq_tiles = cos_r.shape[0] // PROJ_TM
    table = pl.BlockSpec((PROJ_TM, LANES), lambda i, j: (i % seq_tiles, 0))
    merge_tiles = 3 * D_MODEL // PROJ_TN
    bias = pl.BlockSpec((None, 1, PROJ_TN),
                        lambda i, j: (layer, 0, jnp.clip(j - Z_MERGE // PROJ_TN, 0, merge_tiles - 1)))
    kv_tile = REF_ATT_KV // PROJ_TN
    n_tiles = Z_WIDTH // PROJ_TN
    assert REF_ATT_KV % PROJ_TN == 0 and 2 * KV_WIDTH == PROJ_TN

    def w_tile(j):
        return jnp.where(j == n_tiles - 1, kv_tile, jnp.where(j >= kv_tile, j + 1, j))

    return pl.pallas_call(
        _in_proj_kernel,
        out_shape=jax.ShapeDtypeStruct((n, Z_WIDTH), BF16),
        grid=(n // PROJ_TM, n_tiles),
        in_specs=[
            pl.BlockSpec((PROJ_TM, D_MODEL), lambda i, j: (i, 0)),
            pl.BlockSpec((None, 1, D_MODEL), lambda i, j: (layer, 0, 0)),
            pl.BlockSpec((None, D_MODEL, PROJ_TN), lambda i, j: (layer, 0, w_tile(j))),
            table, table, bias,
        ],
        out_specs=pl.BlockSpec((PROJ_TM, PROJ_TN), lambda i, j: (i, j)),
        scratch_shapes=[pltpu.VMEM((PROJ_TM, D_MODEL), BF16)],
        compiler_params=pltpu.CompilerParams(
            dimension_semantics=("arbitrary", "arbitrary"), vmem_limit_bytes=VMEM_LIMIT_BYTES),
        name="in_proj",
    )(x2, norm_g.reshape(depth, 1, D_MODEL), w_in, cos_r, sin_r, b_gate.reshape(depth, 1, 3 * D_MODEL))


def _prep_kernel(aq_ref, akv_ref, cosa_ref, sina_ref, gq_ref, gk_ref, avg_ref, swap_ref,
                 repk_ref, repv_ref, qp_ref, kp_ref, vp_ref, k_ref):
    avg = avg_ref[...]
    swap = swap_ref[...]
    q_scale = LOG2_E * HEAD_DIM ** -0.5

    def chunk(r, carry):
        rows = pl.ds(pl.multiple_of(r * PREP_ROWS, PREP_ROWS), PREP_ROWS)
        cos_a = cosa_ref[rows, :]
        sin_a = sina_ref[rows, :]

        def norm_rope(x, g_ref, scale):
            ms = jnp.dot((x * x).astype(BF16), avg, preferred_element_type=F32)
            xg = x * g_ref[...]
            partner = jnp.dot(xg.astype(BF16), swap, preferred_element_type=F32)
            return (xg * cos_a + partner * sin_a) * (lax.rsqrt(ms + EPS) * scale)

        for c in range(D_MODEL // LANES):
            lanes = slice(c * LANES, (c + 1) * LANES)
            qp_ref[rows, lanes] = norm_rope(aq_ref[rows, lanes].astype(F32), gq_ref, q_scale).astype(BF16)
        for c in range(KV_WIDTH // LANES):
            lanes = slice(c * LANES, (c + 1) * LANES)
            k_ref[rows, lanes] = norm_rope(akv_ref[rows, lanes].astype(F32), gk_ref, 1.0).astype(BF16)
        return carry

    lax.fori_loop(0, PREP_TILE // PREP_ROWS, chunk, 0)
    kp_ref[...] = jnp.dot(k_ref[...], repk_ref[...], preferred_element_type=F32).astype(BF16)
    vp_ref[...] = jnp.dot(akv_ref[:, KV_WIDTH:], repv_ref[...], preferred_element_type=F32).astype(BF16)


def _prep(z, seq, cos_a, sin_a, q_norm_g, k_norm_g):
    n = z.shape[0]
    seq_tiles = seq // PREP_TILE
    heads_per_block = LANES // HEAD_DIM
    gq = jnp.tile(q_norm_g, heads_per_block).reshape(1, LANES)
    gk = jnp.tile(k_norm_g, heads_per_block).reshape(1, LANES)
    lane = jnp.arange(LANES)
    head = lane // HEAD_DIM
    avg = jnp.where(head[:, None] == head[None, :], 1.0 / HEAD_DIM, 0.0).astype(BF16)
    within = lane % HEAD_DIM
    half = ROPE_DIM // 2
    partner = jnp.where(within < half, lane + half, jnp.where(within < ROPE_DIM, lane - half, -1))
    swap = (lane[:, None] == partner[None, :]).astype(BF16)
    src = jnp.arange(KV_WIDTH)

    def rep(slots):
        dst = jnp.arange(ATTN_KV_HEADS * slots * HEAD_DIM)
        return ((src[:, None] // HEAD_DIM == dst[None, :] // (slots * HEAD_DIM))
                & (src[:, None] % HEAD_DIM == dst[None, :] % HEAD_DIM)).astype(BF16)

    def rows(width, col):
        return pl.BlockSpec((PREP_TILE, width), lambda i: (i, col // width))

    def whole(shape):
        return pl.BlockSpec(shape, lambda i: (0, 0))

    tab = pl.BlockSpec((PREP_TILE, LANES), lambda i: (i % seq_tiles, 0))
    wide = jax.ShapeDtypeStruct((n, D_MODEL), BF16)
    wide_spec = pl.BlockSpec((PREP_TILE, D_MODEL), lambda i: (i, 0))
    return pl.pallas_call(
        _prep_kernel,
        out_shape=(wide, wide, jax.ShapeDtypeStruct((n, V_REP_WIDTH), BF16)),
        grid=(n // PREP_TILE,),
        in_specs=[rows(D_MODEL, Z_ATT_Q), rows(2 * KV_WIDTH, Z_ATT_KV),
                  tab, tab, whole((1, LANES)), whole((1, LANES)),
                  whole((LANES, LANES)), whole((LANES, LANES)),
                  whole((KV_WIDTH, D_MODEL)), whole((KV_WIDTH, V_REP_WIDTH))],
        out_specs=(wide_spec, wide_spec, pl.BlockSpec((PREP_TILE, V_REP_WIDTH), lambda i: (i, 0))),
        scratch_shapes=[pltpu.VMEM((PREP_TILE, KV_WIDTH), BF16)],
        compiler_params=pltpu.CompilerParams(
            dimension_semantics=("arbitrary",), vmem_limit_bytes=VMEM_LIMIT_BYTES),
        name="qk_prep",
    )(z, z, cos_a, sin_a, gq, gk, avg, swap, rep(ATTN_GROUP), rep(LANES // HEAD_DIM))


def _conv_kernel(seq_tiles, a_ref, b_ref, gate_ref, ap_ref, bp_ref, an_ref, bn_ref,
                 w_ref, cb_ref, lg_ref, lb_ref, out_ref, v_ref, y_ref):
    t = pl.program_id(0) % seq_tiles

    def glu(a, b):
        return a.astype(F32) * _sigmoid(b.astype(F32))

    def scatter(val, tok0):
        for c in range(CBLK):
            v_ref[pl.ds(tok0 * CBLK + c, val.shape[0], stride=CBLK), :] = val[:, c * LANES:(c + 1) * LANES]

    scatter(jnp.where(t == 0, 0.0, glu(ap_ref[...], bp_ref[...])), 0)
    scatter(jnp.where(t == seq_tiles - 1, 0.0, glu(an_ref[...], bn_ref[...])), CONV_HALO + CONV_TILE)

    def fill(r, carry):
        for u in range(CONV_NORM_UNROLL):
            r0 = pl.multiple_of((r * CONV_NORM_UNROLL + u) * CONV_ROWS, CONV_ROWS)
            rows = pl.ds(r0, CONV_ROWS)
            scatter(glu(a_ref[rows, :], b_ref[rows, :]), r0 + CONV_HALO)
        return carry

    lax.fori_loop(0, CONV_TILE // (CONV_ROWS * CONV_NORM_UNROLL), fill, 0)

    bias = cb_ref[...]

    def conv_tokens(r, carry):
        t0 = r * CONV_TOKENS
        acc = [bias] * CONV_TOKENS
        for k in range(CONV_KERNEL):
            wk = w_ref[k * CBLK:(k + 1) * CBLK, :]
            for i in range(CONV_TOKENS):
                src = pl.multiple_of((t0 + (i + k - CONV_HALF + CONV_HALO)) * CBLK, CBLK)
                acc[i] = acc[i] + v_ref[pl.ds(src, CBLK), :] * wk
        for i in range(CONV_TOKENS):
            y_ref[pl.ds(pl.multiple_of((t0 + i) * CBLK, CBLK), CBLK), :] = acc[i]
        return carry

    lax.fori_loop(0, CONV_TILE // CONV_TOKENS, conv_tokens, 0)

    def norm_rows(r, carry):
        for u in range(CONV_NORM_UNROLL):
            r0 = pl.multiple_of((r * CONV_NORM_UNROLL + u) * CONV_ROWS, CONV_ROWS)
            y = jnp.concatenate(
                [y_ref[pl.ds(r0 * CBLK + c, CONV_ROWS, stride=CBLK), :] for c in range(CBLK)], axis=1)
            mu = jnp.mean(y, axis=-1, keepdims=True)
            d = y - mu
            var = jnp.mean(d * d, axis=-1, keepdims=True)
            yn = d * lax.rsqrt(var + EPS) * lg_ref[...] + lb_ref[...]
            g = gate_ref[pl.ds(r0, CONV_ROWS), :].astype(F32)
            out_ref[pl.ds(r0, CONV_ROWS), :] = (_silu(yn) * _silu(g)).astype(BF16)
        return carry

    lax.fori_loop(0, CONV_TILE // (CONV_ROWS * CONV_NORM_UNROLL), norm_rows, 0)


def _conv_branch(z, seq, conv_dw, conv_b, ln_g, ln_b):
    n = z.shape[0]
    seq_tiles = seq // CONV_TILE
    halo_per_tile = CONV_TILE // CONV_HALO
    last_halo = n // CONV_HALO - 1
    w_tiles = conv_dw.reshape(CONV_KERNEL * CBLK, LANES)

    def main(col):
        return pl.BlockSpec((CONV_TILE, D_MODEL), lambda i: (i, col // D_MODEL))

    def prev(col):
        return pl.BlockSpec((CONV_HALO, D_MODEL),
                            lambda i: (jnp.maximum(i * halo_per_tile - 1, 0), col // D_MODEL))

    def nxt(col):
        return pl.BlockSpec((CONV_HALO, D_MODEL),
                            lambda i: (jnp.minimum((i + 1) * halo_per_tile, last_halo), col // D_MODEL))

    vec = pl.BlockSpec((1, D_MODEL), lambda i: (0, 0))
    return pl.pallas_call(
        functools.partial(_conv_kernel, seq_tiles),
        out_shape=jax.ShapeDtypeStruct((n, D_MODEL), BF16),
        grid=(n // CONV_TILE,),
        in_specs=[main(Z_CONV_A), main(Z_CONV_B), main(Z_CONV_GATE),
                  prev(Z_CONV_A), prev(Z_CONV_B), nxt(Z_CONV_A), nxt(Z_CONV_B),
                  pl.BlockSpec((CONV_KERNEL * CBLK, LANES), lambda i: (0, 0)),
                  pl.BlockSpec((CBLK, LANES), lambda i: (0, 0)), vec, vec],
        out_specs=pl.BlockSpec((CONV_TILE, D_MODEL), lambda i: (i, 0)),
        scratch_shapes=[pltpu.VMEM(((CONV_TILE + 2 * CONV_HALO) * CBLK, LANES), F32),
                        pltpu.VMEM((CONV_TILE * CBLK, LANES), F32)],
        compiler_params=pltpu.CompilerParams(
            dimension_semantics=("arbitrary",), vmem_limit_bytes=VMEM_LIMIT_BYTES),
        name="conv_branch",
    )(z, z, z, z, z, z, z, w_tiles, conv_b.reshape(CBLK, LANES), ln_g.reshape(1, D_MODEL),
      ln_b.reshape(1, D_MODEL))


def _ret_kernel(n_chunks, rd_ref, qf_ref, kf_ref, vf_ref, qb_ref, kb_ref, vb_ref,
                out_ref, oacc_ref, state_ref, dmat_ref, dec_ref, cdec_ref):
    L = RET_CHUNK
    head0 = (pl.program_id(0) % (RET_HEADS // RET_HPS)) * RET_HPS
    c = pl.program_id(1)

    @pl.when(c == 0)
    def _():
        idx = lax.broadcasted_iota(jnp.int32, (L, LANES), 0).astype(F32)
        i = lax.broadcasted_iota(jnp.int32, (L, L), 0)
        j = lax.broadcasted_iota(jnp.int32, (L, L), 1)
        d = (i - j).astype(F32)
        state_ref[...] = jnp.zeros_like(state_ref)
        for hd in range(RET_HPS):
            lg_f = -jnp.exp(jnp.full((L, LANES), rd_ref[0, head0 + hd], F32))
            lg_b = -jnp.exp(jnp.full((L, LANES), rd_ref[1, head0 + hd], F32))
            dec_ref[hd, 0] = jnp.exp(lg_f * (idx + 1.0))
            dec_ref[hd, 1] = jnp.exp(lg_f * (L - 1.0 - idx))
            dec_ref[hd, 2] = jnp.exp(lg_b * (L - idx))
            dec_ref[hd, 3] = jnp.exp(lg_b * idx)
            cdec_ref[hd, 0] = jnp.exp(lg_f[:RET_QK_DIM] * float(L))
            cdec_ref[hd, 1] = jnp.exp(lg_b[:RET_QK_DIM] * float(L))
            lgm_f = -jnp.exp(jnp.full((L, L), rd_ref[0, head0 + hd], F32))
            lgm_b = -jnp.exp(jnp.full((L, L), rd_ref[1, head0 + hd], F32))
            dmat_ref[hd] = jnp.where(i >= j, jnp.exp(lgm_f * jnp.maximum(d, 0.0)),
                                     jnp.exp(lgm_b * jnp.maximum(-d, 0.0)))

    rows_f = pl.ds(pl.multiple_of(c * L, L), L)
    rows_b = pl.ds(pl.multiple_of((n_chunks - 1 - c) * L, L), L)
    parts = []
    for hd in range(RET_HPS):
        qk_cols = slice(hd * RET_QK_DIM, (hd + 1) * RET_QK_DIM)
        v_cols = slice(hd * RET_V_DIM, (hd + 1) * RET_V_DIM)

        def decayed(ref, dec):
            return (ref[:, qk_cols].astype(F32)
                    * jnp.concatenate([dec] * (RET_QK_DIM // LANES), axis=1)).astype(BF16)

        def advance(direction, q_ref, k_ref, v_ref):
            state = state_ref[hd, direction]
            cross = jnp.dot(decayed(q_ref, dec_ref[hd, 2 * direction]), state.astype(BF16),
                            preferred_element_type=F32)
            kv = lax.dot_general(decayed(k_ref, dec_ref[hd, 2 * direction + 1]), v_ref[:, v_cols],
                                 (((0,), (0,)), ((), ())), preferred_element_type=F32)
            chunk_dec = jnp.concatenate([cdec_ref[hd, direction]] * (RET_V_DIM // LANES), axis=1)
            state_ref[hd, direction] = chunk_dec * state + kv
            return cross

        s = lax.dot_general(qf_ref[:, qk_cols], kf_ref[:, qk_cols], (((1,), (1,)), ((), ())),
                            preferred_element_type=F32)
        part_f = jnp.dot((s * dmat_ref[hd]).astype(BF16), vf_ref[:, v_cols], preferred_element_type=F32)
        part_f = part_f + advance(0, qf_ref, kf_ref, vf_ref)
        part_b = advance(1, qb_ref, kb_ref, vb_ref)
        parts.append((v_cols, part_f, part_b))

    @pl.when(c < n_chunks // 2)
    def _():
        for v_cols, part_f, part_b in parts:
            oacc_ref[rows_f, v_cols] = part_f
            oacc_ref[rows_b, v_cols] = part_b

    @pl.when(c >= n_chunks // 2)
    def _():
        for v_cols, part_f, part_b in parts:
            out_ref[0, :, v_cols] = (oacc_ref[rows_f, v_cols] + part_f).astype(BF16)
            out_ref[1, :, v_cols] = (oacc_ref[rows_b, v_cols] + part_b).astype(BF16)


def _ret_branch(z, batch, seq, ret_decay):
    L = RET_CHUNK
    n_chunks = seq // L
    half = n_chunks // 2
    groups = RET_HEADS // RET_HPS
    assert n_chunks % 2 == 0 and RET_HEADS % RET_HPS == 0

    def fwd(bg, c):
        return (bg // groups) * n_chunks + c

    def bwd(bg, c):
        return (bg // groups) * n_chunks + (n_chunks - 1 - c)

    def zcol(col, width, bg):
        return col // width + bg % groups

    qk = (L, RET_HPS * RET_QK_DIM)
    vv = (L, RET_HPS * RET_V_DIM)
    in_specs = [pl.BlockSpec(memory_space=pltpu.SMEM)]
    for row in (fwd, bwd):
        in_specs += [
            pl.BlockSpec(qk, lambda bg, c, row=row: (row(bg, c), zcol(Z_RET_Q, qk[1], bg))),
            pl.BlockSpec(qk, lambda bg, c, row=row: (row(bg, c), zcol(Z_RET_K, qk[1], bg))),
            pl.BlockSpec(vv, lambda bg, c, row=row: (row(bg, c), zcol(Z_RET_V, vv[1], bg))),
        ]
    return pl.pallas_call(
        functools.partial(_ret_kernel, n_chunks),
        out_shape=jax.ShapeDtypeStruct((batch, half, 2, L, RET_HEADS * RET_V_DIM), BF16),
        grid=(batch * groups, n_chunks),
        in_specs=in_specs,
        out_specs=pl.BlockSpec((None, None, 2, L, vv[1]),
                               lambda bg, c: (bg // groups, jnp.maximum(c - half, 0), 0, 0, bg % groups)),
        scratch_shapes=[
            pltpu.VMEM((seq, vv[1]), F32),
            pltpu.VMEM((RET_HPS, 2, RET_QK_DIM, RET_V_DIM), F32),
            pltpu.VMEM((RET_HPS, L, L), F32),
            pltpu.VMEM((RET_HPS, 4, L, LANES), F32),
            pltpu.VMEM((RET_HPS, 2, RET_QK_DIM, LANES), F32),
        ],
        compiler_params=pltpu.CompilerParams(
            dimension_semantics=("arbitrary", "arbitrary"), vmem_limit_bytes=VMEM_LIMIT_BYTES),
        name="retention",
    )(ret_decay, z, z, z, z, z, z)


def _attn_kernel(seq_blocks, sink_ref, q_ref, kl_ref, km_ref, kr_ref, vl_ref, vm_ref, vr_ref,
                 gate_ref, out_ref):
    T = ATTN_BLOCK
    GW = ATTN_GROUP * HEAD_DIM
    blk = pl.program_id(0) % seq_blocks
    lane = lax.broadcasted_iota(jnp.int32, (T, GW), 1)
    head_masks = [lane // HEAD_DIM == h for h in range(ATTN_GROUP)]
    row = lax.broadcasted_iota(jnp.int32, (T, T), 0)
    col = lax.broadcasted_iota(jnp.int32, (T, T), 1)
    mask_l = (col >= row) & (blk > 0)
    mask_r = (col <= row) & (blk < seq_blocks - 1)
    lane_o = lax.broadcasted_iota(jnp.int32, (T, LANES), 1)

    def scores(g):
        lanes = slice(g * GW, (g + 1) * GW)
        qg = q_ref[:, lanes]
        q_heads = jnp.concatenate([jnp.where(m, qg, jnp.zeros_like(qg)) for m in head_masks], axis=0)
        k_win = jnp.concatenate([kl_ref[:, lanes], km_ref[:, lanes], kr_ref[:, lanes]], axis=0)
        return lax.dot_general(q_heads, k_win, (((1,), (1,)), ((), ())), preferred_element_type=F32)

    def softmax(g, s):
        ps, invs = [], []
        for h in range(ATTN_GROUP):
            rows = slice(h * T, (h + 1) * T)
            sl = jnp.where(mask_l, s[rows, :T], -1e30)
            sm = s[rows, T:2 * T]
            sr = jnp.where(mask_r, s[rows, 2 * T:], -1e30)
            sink = sink_ref[g * ATTN_GROUP + h] * LOG2_E
            m = jnp.maximum(jnp.max(jnp.maximum(jnp.maximum(sl, sm), sr), axis=-1, keepdims=True), sink)
            el = jnp.exp2(sl - m)
            em = jnp.exp2(sm - m)
            er = jnp.exp2(sr - m)
            invs.append(1.0 / (jnp.sum(el + em + er, axis=-1, keepdims=True) + jnp.exp2(sink - m)))
            ps.append(jnp.concatenate([el, em, er], axis=1).astype(BF16))
        return ps, invs

    def values(g, ps, invs):
        lanes = slice(g * GW, (g + 1) * GW)
        outs = []
        v_win = jnp.concatenate([v_ref[:, g * LANES:(g + 1) * LANES] for v_ref in (vl_ref, vm_ref, vr_ref)],
                                axis=0)
        for hb in range(GW // LANES):
            o2 = jnp.dot(jnp.concatenate(ps[2 * hb:2 * hb + 2], axis=0), v_win, preferred_element_type=F32)
            outs.append(jnp.where(lane_o < HEAD_DIM, o2[:T] * invs[2 * hb], o2[T:] * invs[2 * hb + 1]))
        gate = gate_ref[:, lanes].astype(F32)
        out_ref[:, lanes] = (jnp.concatenate(outs, axis=1) * _silu(gate)).astype(BF16)

    ss = [scores(g) for g in range(ATTN_KV_HEADS)]
    ps = [softmax(g, ss[g]) for g in range(ATTN_KV_HEADS)]
    for g in range(ATTN_KV_HEADS):
        values(g, *ps[g])


def _attn_branch(z, qp, kp, vp, seq, attn_sink):
    n = z.shape[0]
    T = ATTN_BLOCK
    seq_blocks = seq // T
    last = n // T - 1
    blk = (T, D_MODEL)

    def here(col=0):
        return pl.BlockSpec(blk, lambda i: (i, col // D_MODEL))

    def window(width):
        return [pl.BlockSpec((T, width), lambda i: (jnp.maximum(i - 1, 0), 0)),
                pl.BlockSpec((T, width), lambda i: (i, 0)),
                pl.BlockSpec((T, width), lambda i: (jnp.minimum(i + 1, last), 0))]

    return pl.pallas_call(
        functools.partial(_attn_kernel, seq_blocks),
        out_shape=jax.ShapeDtypeStruct((n, D_MODEL), BF16),
        grid=(n // T,),
        in_specs=[pl.BlockSpec(memory_space=pltpu.SMEM), here()] + window(D_MODEL) + window(V_REP_WIDTH)
        + [here(Z_ATT_GATE)],
        out_specs=here(),
        compiler_params=pltpu.CompilerParams(
            dimension_semantics=("arbitrary",), vmem_limit_bytes=VMEM_LIMIT_BYTES),
        name="window_attn",
    )(attn_sink, qp, kp, kp, kp, vp, vp, vp, z)


def _out_kernel(x_ref, uc_ref, o_ref, rg0_ref, rg1_ref, ua_ref, g0_ref, g1_ref, g2_ref,
                wc_ref, wr_ref, wa_ref, wo_ref, out_ref, ur_ref):
    def gate(g_ref):
        return g_ref[...].astype(F32)

    for r in range(OUT_TM // OUT_NORM_ROWS):
        rows = slice(r * OUT_NORM_ROWS, (r + 1) * OUT_NORM_ROWS)
        for h in range(RET_HEADS):
            cols = slice(h * RET_V_DIM, (h + 1) * RET_V_DIM)
            o = o_ref[rows, cols].astype(F32)
            mu = jnp.mean(o, axis=-1, keepdims=True)
            d = o - mu
            var = jnp.mean(d * d, axis=-1, keepdims=True)
            rg_ref = (rg0_ref, rg1_ref)[h * RET_V_DIM // D_MODEL]
            lo = h * RET_V_DIM % D_MODEL
            g = rg_ref[rows, lo:lo + RET_V_DIM].astype(F32)
            ur_ref[rows, cols] = (d * lax.rsqrt(var + EPS) * _silu(g)).astype(BF16)

    merged = gate(g0_ref) * jnp.dot(uc_ref[...], wc_ref[...], preferred_element_type=F32)
    merged = merged + gate(g1_ref) * jnp.dot(ur_ref[...], wr_ref[...], preferred_element_type=F32)
    merged = merged + gate(g2_ref) * jnp.dot(ua_ref[...], wa_ref[...], preferred_element_type=F32)
    out_ref[...] = x_ref[...] + jnp.dot(merged.astype(BF16), wo_ref[...], preferred_element_type=F32)


def _out_proj(x2, u_conv, u_ret, u_attn, z, seq, w_conv_out, w_ret_out, w_attn_out, w_out, layer):
    n = x2.shape[0]
    n_chunks = seq // OUT_TM
    half = n_chunks // 2

    def rows(width, col=0):
        return pl.BlockSpec((OUT_TM, width), lambda i: (i, col // width))

    def whole(shape):
        return pl.BlockSpec((None,) + shape, lambda i: (layer, 0, 0))

    def ret_index(i):
        b, c = i // n_chunks, i % n_chunks
        upper = c >= half
        return (b, jnp.where(upper, c - half, half - 1 - c), jnp.where(upper, 0, 1), 0, 0)

    return pl.pallas_call(
        _out_kernel,
        out_shape=jax.ShapeDtypeStruct((n, D_MODEL), F32),
        grid=(n // OUT_TM,),
        in_specs=[rows(D_MODEL), rows(D_MODEL),
                  pl.BlockSpec((None, None, None, OUT_TM, 2 * D_MODEL), ret_index),
                  rows(D_MODEL, Z_RET_GATE), rows(D_MODEL, Z_RET_GATE + D_MODEL),
                  rows(D_MODEL),
                  rows(D_MODEL, Z_MERGE), rows(D_MODEL, Z_MERGE + D_MODEL), rows(D_MODEL, Z_MERGE + 2 * D_MODEL),
                  whole((D_MODEL, D_MODEL)), whole((2 * D_MODEL, D_MODEL)),
                  whole((D_MODEL, D_MODEL)), whole((D_MODEL, D_MODEL))],
        out_specs=rows(D_MODEL),
        scratch_shapes=[pltpu.VMEM((OUT_TM, RET_HEADS * RET_V_DIM), BF16)],
        compiler_params=pltpu.CompilerParams(
            dimension_semantics=("arbitrary",), vmem_limit_bytes=VMEM_LIMIT_BYTES),
        name="out_proj",
    )(x2, u_conv, u_ret, z, z, u_attn, z, z, z, w_conv_out, w_ret_out, w_attn_out, w_out)


def _rotary_tables(seq):
    pos = jnp.arange(seq, dtype=jnp.int32).astype(F32)
    half_r = RET_QK_DIM // 2
    inv_r = RET_THETA ** (-jnp.arange(half_r, dtype=F32) / half_r)
    ang_r = pos[:, None] * inv_r[None, :]
    half_a = ROPE_DIM // 2
    inv_a = ROPE_THETA ** (-jnp.arange(half_a, dtype=F32) / half_a)
    ang_a = pos[:, None] * inv_a[None, :]
    ca, sa = jnp.cos(ang_a), jnp.sin(ang_a)
    rest = HEAD_DIM - ROPE_DIM
    cos64 = jnp.concatenate([ca, ca, jnp.ones((seq, rest), F32)], axis=1)
    sin64 = jnp.concatenate([-sa, sa, jnp.zeros((seq, rest), F32)], axis=1)
    rep = LANES // HEAD_DIM
    return jnp.cos(ang_r), jnp.sin(ang_r), jnp.tile(cos64, (1, rep)), jnp.tile(sin64, (1, rep))


def kernel(x, norm_g, w_in, b_gate, conv_dw, conv_b, conv_ln_g, conv_ln_b, ret_decay,
           q_norm_g, k_norm_g, attn_sink, w_conv_out, w_ret_out, w_attn_out, w_out):
    batch, seq, d = x.shape
    depth = norm_g.shape[0]
    assert d == D_MODEL and seq % max(CONV_TILE, 2 * RET_CHUNK, PREP_TILE, ATTN_BLOCK) == 0
    assert (batch * seq) % PROJ_TM == 0 and w_in.shape[-1] == Z_WIDTH
    cos_r, sin_r, cos_a, sin_a = _rotary_tables(seq)
    x2 = x.reshape(batch * seq, d)
    out_weights = [w.astype(BF16) for w in (w_conv_out, w_ret_out, w_attn_out, w_out)]
    for l in range(depth):
        z = _in_proj(x2, norm_g, w_in, b_gate, cos_r, sin_r, l)
        qp, kp, vp = _prep(z, seq, cos_a, sin_a, q_norm_g[l], k_norm_g[l])
        u_conv = _conv_branch(z, seq, conv_dw[l], conv_b[l], conv_ln_g[l], conv_ln_b[l])
        u_ret = _ret_branch(z, batch, seq, ret_decay[l])
        u_attn = _attn_branch(z, qp, kp, vp, seq, attn_sink[l])
        x2 = _out_proj(x2, u_conv, u_ret, u_attn, z, seq, *out_weights, l)
    return x2.reshape(batch, seq, d)
```

```python
import functools
import math

import jax
import jax.numpy as jnp
from jax import lax
from jax.experimental import pallas as pl
from jax.experimental.pallas import tpu as pltpu

F32 = jnp.float32
BF16 = jnp.bfloat16

D_MODEL = 1024
HEAD_DIM = 64
EPS = 1e-6
CONV_KERNEL = 31
CONV_HALF = CONV_KERNEL // 2
RET_HEADS = 4
RET_QK_DIM = 256
RET_V_DIM = 512
RET_THETA = 10000.0
ATTN_HEADS = 16
ATTN_KV_HEADS = 4
ATTN_GROUP = ATTN_HEADS // ATTN_KV_HEADS
ATTN_BLOCK = 128
ROPE_THETA = 500000.0
ROPE_DIM = HEAD_DIM // 4
KV_WIDTH = ATTN_KV_HEADS * HEAD_DIM
LOG2_E = math.log2(math.e)

LANES = 128
VMEM_LIMIT_BYTES = 56 * 1024 * 1024

Z_CONV_A, Z_CONV_B, Z_CONV_GATE = 0, 1024, 2048
Z_RET_Q, Z_RET_K, Z_RET_V, Z_RET_GATE = 3072, 4096, 5120, 7168
Z_ATT_Q, Z_ATT_GATE, Z_MERGE, Z_ATT_KV = 9216, 10240, 11264, 14336
Z_WIDTH = Z_ATT_KV + 2 * KV_WIDTH
REF_ATT_KV = Z_ATT_Q + D_MODEL

PROJ_TM, PROJ_TN, PROJ_ROT_ROWS = 2048, 512, 256
NORM_ROWS = 256
CONV_TILE, CONV_ROWS, CONV_HALO, CONV_TOKENS, CONV_NORM_UNROLL = 512, 32, 16, 32, 4
CBLK = D_MODEL // LANES
RET_CHUNK = 256
RET_HPS = 2
PREP_TILE, PREP_ROWS = 512, 128
V_REP_WIDTH = ATTN_KV_HEADS * LANES
OUT_TM = RET_CHUNK
OUT_NORM_ROWS = 64


def _sigmoid(x):
    return 1.0 / (1.0 + jnp.exp(-x))


def _silu(x):
    return x * _sigmoid(x)


def _in_proj_kernel(x_ref, g_ref, w_ref, cos_ref, sin_ref, bg_ref, z_ref, h_ref):
    j = pl.program_id(1)

    @pl.when(j == 0)
    def _():
        def norm_rows(r, carry):
            rows = pl.ds(pl.multiple_of(r * NORM_ROWS, NORM_ROWS), NORM_ROWS)
            xv = x_ref[rows, :]
            ms = jnp.mean(xv * xv, axis=-1, keepdims=True)
            h_ref[rows, :] = (xv * lax.rsqrt(ms + EPS) * g_ref[...]).astype(BF16)
            return carry

        lax.fori_loop(0, PROJ_TM // NORM_ROWS, norm_rows, 0)

    def product():
        return jnp.dot(h_ref[...], w_ref[...].astype(BF16), preferred_element_type=F32)

    rotary_tile = (j >= Z_RET_Q // PROJ_TN) & (j < Z_RET_V // PROJ_TN)
    merge_tile = (j >= Z_MERGE // PROJ_TN) & (j < Z_ATT_KV // PROJ_TN)

    @pl.when(jnp.logical_not(rotary_tile | merge_tile))
    def _():
        z_ref[...] = product().astype(BF16)

    @pl.when(merge_tile)
    def _():
        w = w_ref[...].astype(BF16)
        for r in range(PROJ_TM // PROJ_ROT_ROWS):
            rows = slice(r * PROJ_ROT_ROWS, (r + 1) * PROJ_ROT_ROWS)
            acc = jnp.dot(h_ref[rows, :], w, preferred_element_type=F32)
            z_ref[rows, :] = (0.5 * jnp.tanh(0.5 * (acc + bg_ref[...])) + 0.5).astype(BF16)

    @pl.when(rotary_tile)
    def _():
        scale = jnp.where(j >= Z_RET_K // PROJ_TN, RET_QK_DIM ** -0.5, 1.0)
        half = RET_QK_DIM // 2
        w = w_ref[...].astype(BF16)
        for r in range(PROJ_TM // PROJ_ROT_ROWS):
            rows = slice(r * PROJ_ROT_ROWS, (r + 1) * PROJ_ROT_ROWS)
            acc = jnp.dot(h_ref[rows, :], w, preferred_element_type=F32)
            c = cos_ref[rows, :] * scale
            s = sin_ref[rows, :] * scale
            for hd in range(PROJ_TN // RET_QK_DIM):
                lo = slice(hd * RET_QK_DIM, hd * RET_QK_DIM + half)
                hi = slice(hd * RET_QK_DIM + half, (hd + 1) * RET_QK_DIM)
                z_ref[rows, lo] = (acc[:, lo] * c - acc[:, hi] * s).astype(BF16)
                z_ref[rows, hi] = (acc[:, hi] * c + acc[:, lo] * s).astype(BF16)


def _in_proj(x2, norm_g, w_in, b_gate, cos_r, sin_r, layer):
    n = x2.shape[0]
    depth = norm_g.shape[0]
    seq_tiles = cos_r.shape[0] // PROJ_TM
    table = pl.BlockSpec((PROJ_TM, LANES), lambda i, j: (i % seq_tiles, 0))
    merge_tiles = 3 * D_MODEL // PROJ_TN
    bias = pl.BlockSpec((None, 1, PROJ_TN),
                        lambda i, j: (layer, 0, jnp.clip(j - Z_MERGE // PROJ_TN, 0, merge_tiles - 1)))
    kv_tile = REF_ATT_KV // PROJ_TN
    n_tiles = Z_WIDTH // PROJ_TN
    assert REF_ATT_KV % PROJ_TN == 0 and 2 * KV_WIDTH == PROJ_TN

    def w_tile(j):
        return jnp.where(j == n_tiles - 1, kv_tile, jnp.where(j >= kv_tile, j + 1, j))

    return pl.pallas_call(
        _in_proj_kernel,
        out_shape=jax.ShapeDtypeStruct((n, Z_WIDTH), BF16),
        grid=(n // PROJ_TM, n_tiles),
        in_specs=[
            pl.BlockSpec((PROJ_TM, D_MODEL), lambda i, j: (i, 0)),
            pl.BlockSpec((None, 1, D_MODEL), lambda i, j: (layer, 0, 0)),
            pl.BlockSpec((None, D_MODEL, PROJ_TN), lambda i, j: (layer, 0, w_tile(j))),
            table, table, bias,
        ],
        out_specs=pl.BlockSpec((PROJ_TM, PROJ_TN), lambda i, j: (i, j)),
        scratch_shapes=[pltpu.VMEM((PROJ_TM, D_MODEL), BF16)],
        compiler_params=pltpu.CompilerParams(
            dimension_semantics=("arbitrary", "arbitrary"), vmem_limit_bytes=VMEM_LIMIT_BYTES),
        name="in_proj",
    )(x2, norm_g.reshape(depth, 1, D_MODEL), w_in, cos_r, sin_r, b_gate.reshape(depth, 1, 3 * D_MODEL))


def _prep_kernel(aq_ref, akv_ref, cosa_ref, sina_ref, gq_ref, gk_ref, avg_ref, swap_ref,
                 repk_ref, repv_ref, qp_ref, kp_ref, vp_ref, k_ref):
    avg = avg_ref[...]
    swap = swap_ref[...]
    q_scale = LOG2_E * HEAD_DIM ** -0.5

    def chunk(r, carry):
        rows = pl.ds(pl.multiple_of(r * PREP_ROWS, PREP_ROWS), PREP_ROWS)
        cos_a = cosa_ref[rows, :]
        sin_a = sina_ref[rows, :]

        def norm_rope(x, g_ref, scale):
            ms = jnp.dot((x * x).astype(BF16), avg, preferred_element_type=F32)
            xg = x * g_ref[...]
            partner = jnp.dot(xg.astype(BF16), swap, preferred_element_type=F32)
            return (xg * cos_a + partner * sin_a) * (lax.rsqrt(ms + EPS) * scale)

        for c in range(D_MODEL // LANES):
            lanes = slice(c * LANES, (c + 1) * LANES)
            qp_ref[rows, lanes] = norm_rope(aq_ref[rows, lanes].astype(F32), gq_ref, q_scale).astype(BF16)
        for c in range(KV_WIDTH // LANES):
            lanes = slice(c * LANES, (c + 1) * LANES)
            k_ref[rows, lanes] = norm_rope(akv_ref[rows, lanes].astype(F32), gk_ref, 1.0).astype(BF16)
        return carry

    lax.fori_loop(0, PREP_TILE // PREP_ROWS, chunk, 0)
    kp_ref[...] = jnp.dot(k_ref[...], repk_ref[...], preferred_element_type=F32).astype(BF16)
    vp_ref[...] = jnp.dot(akv_ref[:, KV_WIDTH:], repv_ref[...], preferred_element_type=F32).astype(BF16)


def _prep(z, seq, cos_a, sin_a, q_norm_g, k_norm_g):
    n = z.shape[0]
    seq_tiles = seq // PREP_TILE
    heads_per_block = LANES // HEAD_DIM
    gq = jnp.tile(q_norm_g, heads_per_block).reshape(1, LANES)
    gk = jnp.tile(k_norm_g, heads_per_block).reshape(1, LANES)
    lane = jnp.arange(LANES)
    head = lane // HEAD_DIM
    avg = jnp.where(head[:, None] == head[None, :], 1.0 / HEAD_DIM, 0.0).astype(BF16)
    within = lane % HEAD_DIM
    half = ROPE_DIM // 2
    partner = jnp.where(within < half, lane + half, jnp.where(within < ROPE_DIM, lane - half, -1))
    swap = (lane[:, None] == partner[None, :]).astype(BF16)
    src = jnp.arange(KV_WIDTH)

    def rep(slots):
        dst = jnp.arange(ATTN_KV_HEADS * slots * HEAD_DIM)
        return ((src[:, None] // HEAD_DIM == dst[None, :] // (slots * HEAD_DIM))
                & (src[:, None] % HEAD_DIM == dst[None, :] % HEAD_DIM)).astype(BF16)

    def rows(width, col):
        return pl.BlockSpec((PREP_TILE, width), lambda i: (i, col // width))

    def whole(shape):
        return pl.BlockSpec(shape, lambda i: (0, 0))

    tab = pl.BlockSpec((PREP_TILE, LANES), lambda i: (i % seq_tiles, 0))
    wide = jax.ShapeDtypeStruct((n, D_MODEL), BF16)
    wide_spec = pl.BlockSpec((PREP_TILE, D_MODEL), lambda i: (i, 0))
    return pl.pallas_call(
        _prep_kernel,
        out_shape=(wide, wide, jax.ShapeDtypeStruct((n, V_REP_WIDTH), BF16)),
        grid=(n // PREP_TILE,),
        in_specs=[rows(D_MODEL, Z_ATT_Q), rows(2 * KV_WIDTH, Z_ATT_KV),
                  tab, tab, whole((1, LANES)), whole((1, LANES)),
                  whole((LANES, LANES)), whole((LANES, LANES)),
                  whole((KV_WIDTH, D_MODEL)), whole((KV_WIDTH, V_REP_WIDTH))],
        out_specs=(wide_spec, wide_spec, pl.BlockSpec((PREP_TILE, V_REP_WIDTH), lambda i: (i, 0))),
        scratch_shapes=[pltpu.VMEM((PREP_TILE, KV_WIDTH), BF16)],
        compiler_params=pltpu.CompilerParams(
            dimension_semantics=("arbitrary",), vmem_limit_bytes=VMEM_LIMIT_BYTES),
        name="qk_prep",
    )(z, z, cos_a, sin_a, gq, gk, avg, swap, rep(ATTN_GROUP), rep(LANES // HEAD_DIM))


def _conv_kernel(seq_tiles, a_ref, b_ref, gate_ref, ap_ref, bp_ref, an_ref, bn_ref,
                 w_ref, cb_ref, lg_ref, lb_ref, out_ref, v_ref, y_ref):
    t = pl.program_id(0) % seq_tiles

    def glu(a, b):
        return a.astype(F32) * _sigmoid(b.astype(F32))

    def scatter(val, tok0):
        for c in range(CBLK):
            v_ref[pl.ds(tok0 * CBLK + c, val.shape[0], stride=CBLK), :] = val[:, c * LANES:(c + 1) * LANES]

    scatter(jnp.where(t == 0, 0.0, glu(ap_ref[...], bp_ref[...])), 0)
    scatter(jnp.where(t == seq_tiles - 1, 0.0, glu(an_ref[...], bn_ref[...])), CONV_HALO + CONV_TILE)

    def fill(r, carry):
        for u in range(CONV_NORM_UNROLL):
            r0 = pl.multiple_of((r * CONV_NORM_UNROLL + u) * CONV_ROWS, CONV_ROWS)
            rows = pl.ds(r0, CONV_ROWS)
            scatter(glu(a_ref[rows, :], b_ref[rows, :]), r0 + CONV_HALO)
        return carry

    lax.fori_loop(0, CONV_TILE // (CONV_ROWS * CONV_NORM_UNROLL), fill, 0)

    bias = cb_ref[...]

    def conv_tokens(r, carry):
        t0 = r * CONV_TOKENS
        acc = [bias] * CONV_TOKENS
        for k in range(CONV_KERNEL):
            wk = w_ref[k * CBLK:(k + 1) * CBLK, :]
            for i in range(CONV_TOKENS):
                src = pl.multiple_of((t0 + (i + k - CONV_HALF + CONV_HALO)) * CBLK, CBLK)
                acc[i] = acc[i] + v_ref[pl.ds(src, CBLK), :] * wk
        for i in range(CONV_TOKENS):
            y_ref[pl.ds(pl.multiple_of((t0 + i) * CBLK, CBLK), CBLK), :] = acc[i]
        return carry

    lax.fori_loop(0, CONV_TILE // CONV_TOKENS, conv_tokens, 0)

    def norm_rows(r, carry):
        for u in range(CONV_NORM_UNROLL):
            r0 = pl.multiple_of((r * CONV_NORM_UNROLL + u) * CONV_ROWS, CONV_ROWS)
            y = jnp.concatenate(
                [y_ref[pl.ds(r0 * CBLK + c, CONV_ROWS, stride=CBLK), :] for c in range(CBLK)], axis=1)
            mu = jnp.mean(y, axis=-1, keepdims=True)
            d = y - mu
            var = jnp.mean(d * d, axis=-1, keepdims=True)
            yn = d * lax.rsqrt(var + EPS) * lg_ref[...] + lb_ref[...]
            g = gate_ref[pl.ds(r0, CONV_ROWS), :].astype(F32)
            out_ref[pl.ds(r0, CONV_ROWS), :] = (_silu(yn) * _silu(g)).astype(BF16)
        return carry

    lax.fori_loop(0, CONV_TILE // (CONV_ROWS * CONV_NORM_UNROLL), norm_rows, 0)


def _conv_branch(z, seq, conv_dw, conv_b, ln_g, ln_b):
    n = z.shape[0]
    seq_tiles = seq // CONV_TILE
    halo_per_tile = CONV_TILE // CONV_HALO
    last_halo = n // CONV_HALO - 1
    w_tiles = conv_dw.reshape(CONV_KERNEL * CBLK, LANES)

    def main(col):
        return pl.BlockSpec((CONV_TILE, D_MODEL), lambda i: (i, col // D_MODEL))

    def prev(col):
        return pl.BlockSpec((CONV_HALO, D_MODEL),
                            lambda i: (jnp.maximum(i * halo_per_tile - 1, 0), col // D_MODEL))

    def nxt(col):
        return pl.BlockSpec((CONV_HALO, D_MODEL),
                            lambda i: (jnp.minimum((i + 1) * halo_per_tile, last_halo), col // D_MODEL))

    vec = pl.BlockSpec((1, D_MODEL), lambda i: (0, 0))
    return pl.pallas_call(
        functools.partial(_conv_kernel, seq_tiles),
        out_shape=jax.ShapeDtypeStruct((n, D_MODEL), BF16),
        grid=(n // CONV_TILE,),
        in_specs=[main(Z_CONV_A), main(Z_CONV_B), main(Z_CONV_GATE),
                  prev(Z_CONV_A), prev(Z_CONV_B), nxt(Z_CONV_A), nxt(Z_CONV_B),
                  pl.BlockSpec((CONV_KERNEL * CBLK, LANES), lambda i: (0, 0)),
                  pl.BlockSpec((CBLK, LANES), lambda i: (0, 0)), vec, vec],
        out_specs=pl.BlockSpec((CONV_TILE, D_MODEL), lambda i: (i, 0)),
        scratch_shapes=[pltpu.VMEM(((CONV_TILE + 2 * CONV_HALO) * CBLK, LANES), F32),
                        pltpu.VMEM((CONV_TILE * CBLK, LANES), F32)],
        compiler_params=pltpu.CompilerParams(
            dimension_semantics=("arbitrary",), vmem_limit_bytes=VMEM_LIMIT_BYTES),
        name="conv_branch",
    )(z, z, z, z, z, z, z, w_tiles, conv_b.reshape(CBLK, LANES), ln_g.reshape(1, D_MODEL),
      ln_b.reshape(1, D_MODEL))


def _ret_kernel(n_chunks, rd_ref, qf_ref, kf_ref, vf_ref, qb_ref, kb_ref, vb_ref,
                out_ref, oacc_ref, state_ref, dmat_ref, dec_ref, cdec_ref):
    L = RET_CHUNK
    head0 = (pl.program_id(0) % (RET_HEADS // RET_HPS)) * RET_HPS
    c = pl.program_id(1)

    @pl.when(c == 0)
    def _():
        idx = lax.broadcasted_iota(jnp.int32, (L, LANES), 0).astype(F32)
        i = lax.broadcasted_iota(jnp.int32, (L, L), 0)
        j = lax.broadcasted_iota(jnp.int32, (L, L), 1)
        d = (i - j).astype(F32)
        state_ref[...] = jnp.zeros_like(state_ref)
        for hd in range(RET_HPS):
            lg_f = -jnp.exp(jnp.full((L, LANES), rd_ref[0, head0 + hd], F32))
            lg_b = -jnp.exp(jnp.full((L, LANES), rd_ref[1, head0 + hd], F32))
            dec_ref[hd, 0] = jnp.exp(lg_f * (idx + 1.0))
            dec_ref[hd, 1] = jnp.exp(lg_f * (L - 1.0 - idx))
            dec_ref[hd, 2] = jnp.exp(lg_b * (L - idx))
            dec_ref[hd, 3] = jnp.exp(lg_b * idx)
            cdec_ref[hd, 0] = jnp.exp(lg_f[:RET_QK_DIM] * float(L))
            cdec_ref[hd, 1] = jnp.exp(lg_b[:RET_QK_DIM] * float(L))
            lgm_f = -jnp.exp(jnp.full((L, L), rd_ref[0, head0 + hd], F32))
            lgm_b = -jnp.exp(jnp.full((L, L), rd_ref[1, head0 + hd], F32))
            dmat_ref[hd] = jnp.where(i >= j, jnp.exp(lgm_f * jnp.maximum(d, 0.0)),
                                     jnp.exp(lgm_b * jnp.maximum(-d, 0.0)))

    rows_f = pl.ds(pl.multiple_of(c * L, L), L)
    rows_b = pl.ds(pl.multiple_of((n_chunks - 1 - c) * L, L), L)
    parts = []
    for hd in range(RET_HPS):
        qk_cols = slice(hd * RET_QK_DIM, (hd + 1) * RET_QK_DIM)
        v_cols = slice(hd * RET_V_DIM, (hd + 1) * RET_V_DIM)

        def decayed(ref, dec):
            return (ref[:, qk_cols].astype(F32)
                    * jnp.concatenate([dec] * (RET_QK_DIM // LANES), axis=1)).astype(BF16)

        def advance(direction, q_ref, k_ref, v_ref):
            state = state_ref[hd, direction]
            cross = jnp.dot(decayed(q_ref, dec_ref[hd, 2 * direction]), state.astype(BF16),
                            preferred_element_type=F32)
            kv = lax.dot_general(decayed(k_ref, dec_ref[hd, 2 * direction + 1]), v_ref[:, v_cols],
                                 (((0,), (0,)), ((), ())), preferred_element_type=F32)
            chunk_dec = jnp.concatenate([cdec_ref[hd, direction]] * (RET_V_DIM // LANES), axis=1)
            state_ref[hd, direction] = chunk_dec * state + kv
            return cross

        s = lax.dot_general(qf_ref[:, qk_cols], kf_ref[:, qk_cols], (((1,), (1,)), ((), ())),
                            preferred_element_type=F32)
        part_f = jnp.dot((s * dmat_ref[hd]).astype(BF16), vf_ref[:, v_cols], preferred_element_type=F32)
        part_f = part_f + advance(0, qf_ref, kf_ref, vf_ref)
        part_b = advance(1, qb_ref, kb_ref, vb_ref)
        parts.append((v_cols, part_f, part_b))

    @pl.when(c < n_chunks // 2)
    def _():
        for v_cols, part_f, part_b in parts:
            oacc_ref[rows_f, v_cols] = part_f
            oacc_ref[rows_b, v_cols] = part_b

    @pl.when(c >= n_chunks // 2)
    def _():
        for v_cols, part_f, part_b in parts:
            out_ref[0, :, v_cols] = (oacc_ref[rows_f, v_cols] + part_f).astype(BF16)
            out_ref[1, :, v_cols] = (oacc_ref[rows_b, v_cols] + part_b).astype(BF16)


def _ret_branch(z, batch, seq, ret_decay):
    L = RET_CHUNK
    n_chunks = seq // L
    half = n_chunks // 2
    groups = RET_HEADS // RET_HPS
    assert n_chunks % 2 == 0 and RET_HEADS % RET_HPS == 0

    def fwd(bg, c):
        return (bg // groups) * n_chunks + c

    def bwd(bg, c):
        return (bg // groups) * n_chunks + (n_chunks - 1 - c)

    def zcol(col, width, bg):
        return col // width + bg % groups

    qk = (L, RET_HPS * RET_QK_DIM)
    vv = (L, RET_HPS * RET_V_DIM)
    in_specs = [pl.BlockSpec(memory_space=pltpu.SMEM)]
    for row in (fwd, bwd):
        in_specs += [
            pl.BlockSpec(qk, lambda bg, c, row=row: (row(bg, c), zcol(Z_RET_Q, qk[1], bg))),
            pl.BlockSpec(qk, lambda bg, c, row=row: (row(bg, c), zcol(Z_RET_K, qk[1], bg))),
            pl.BlockSpec(vv, lambda bg, c, row=row: (row(bg, c), zcol(Z_RET_V, vv[1], bg))),
        ]
    return pl.pallas_call(
        functools.partial(_ret_kernel, n_chunks),
        out_shape=jax.ShapeDtypeStruct((batch, half, 2, L, RET_HEADS * RET_V_DIM), BF16),
        grid=(batch * groups, n_chunks),
        in_specs=in_specs,
        out_specs=pl.BlockSpec((None, None, 2, L, vv[1]),
                               lambda bg, c: (bg // groups, jnp.maximum(c - half, 0), 0, 0, bg % groups)),
        scratch_shapes=[
            pltpu.VMEM((seq, vv[1]), F32),
            pltpu.VMEM((RET_HPS, 2, RET_QK_DIM, RET_V_DIM), F32),
            pltpu.VMEM((RET_HPS, L, L), F32),
            pltpu.VMEM((RET_HPS, 4, L, LANES), F32),
            pltpu.VMEM((RET_HPS, 2, RET_QK_DIM, LANES), F32),
        ],
        compiler_params=pltpu.CompilerParams(
            dimension_semantics=("arbitrary", "arbitrary"), vmem_limit_bytes=VMEM_LIMIT_BYTES),
        name="retention",
    )(ret_decay, z, z, z, z, z, z)


def _attn_kernel(seq_blocks, sink_ref, q_ref, kl_ref, km_ref, kr_ref, vl_ref, vm_ref, vr_ref,
                 gate_ref, out_ref):
    T = ATTN_BLOCK
    GW = ATTN_GROUP * HEAD_DIM
    blk = pl.program_id(0) % seq_blocks
    lane = lax.broadcasted_iota(jnp.int32, (T, GW), 1)
    head_masks = [lane // HEAD_DIM == h for h in range(ATTN_GROUP)]
    row = lax.broadcasted_iota(jnp.int32, (T, T), 0)
    col = lax.broadcasted_iota(jnp.int32, (T, T), 1)
    mask_l = (col >= row) & (blk > 0)
    mask_r = (col <= row) & (blk < seq_blocks - 1)
    lane_o = lax.broadcasted_iota(jnp.int32, (T, LANES), 1)

    def scores(g):
        lanes = slice(g * GW, (g + 1) * GW)
        qg = q_ref[:, lanes]
        q_heads = jnp.concatenate([jnp.where(m, qg, jnp.zeros_like(qg)) for m in head_masks], axis=0)
        k_win = jnp.concatenate([kl_ref[:, lanes], km_ref[:, lanes], kr_ref[:, lanes]], axis=0)
        return lax.dot_general(q_heads, k_win, (((1,), (1,)), ((), ())), preferred_element_type=F32)

    def softmax(g, s):
        ps, invs = [], []
        for h in range(ATTN_GROUP):
            rows = slice(h * T, (h + 1) * T)
            sl = jnp.where(mask_l, s[rows, :T], -1e30)
            sm = s[rows, T:2 * T]
            sr = jnp.where(mask_r, s[rows, 2 * T:], -1e30)
            sink = sink_ref[g * ATTN_GROUP + h] * LOG2_E
            m = jnp.maximum(jnp.max(jnp.maximum(jnp.maximum(sl, sm), sr), axis=-1, keepdims=True), sink)
            el = jnp.exp2(sl - m)
            em = jnp.exp2(sm - m)
            er = jnp.exp2(sr - m)
            invs.append(1.0 / (jnp.sum(el + em + er, axis=-1, keepdims=True) + jnp.exp2(sink - m)))
            ps.append(jnp.concatenate([el, em, er], axis=1).astype(BF16))
        return ps, invs

    def values(g, ps, invs):
        lanes = slice(g * GW, (g + 1) * GW)
        outs = []
        v_win = jnp.concatenate([v_ref[:, g * LANES:(g + 1) * LANES] for v_ref in (vl_ref, vm_ref, vr_ref)],
                                axis=0)
        for hb in range(GW // LANES):
            o2 = jnp.dot(jnp.concatenate(ps[2 * hb:2 * hb + 2], axis=0), v_win, preferred_element_type=F32)
            outs.append(jnp.where(lane_o < HEAD_DIM, o2[:T] * invs[2 * hb], o2[T:] * invs[2 * hb + 1]))
        gate = gate_ref[:, lanes].astype(F32)
        out_ref[:, lanes] = (jnp.concatenate(outs, axis=1) * _silu(gate)).astype(BF16)

    ss = [scores(g) for g in range(ATTN_KV_HEADS)]
    ps = [softmax(g, ss[g]) for g in range(ATTN_KV_HEADS)]
    for g in range(ATTN_KV_HEADS):
        values(g, *ps[g])


def _attn_branch(z, qp, kp, vp, seq, attn_sink):
    n = z.shape[0]
    T = ATTN_BLOCK
    seq_blocks = seq // T
    last = n // T - 1
    blk = (T, D_MODEL)

    def here(col=0):
        return pl.BlockSpec(blk, lambda i: (i, col // D_MODEL))

    def window(width):
        return [pl.BlockSpec((T, width), lambda i: (jnp.maximum(i - 1, 0), 0)),
                pl.BlockSpec((T, width), lambda i: (i, 0)),
                pl.BlockSpec((T, width), lambda i: (jnp.minimum(i + 1, last), 0))]

    return pl.pallas_call(
        functools.partial(_attn_kernel, seq_blocks),
        out_shape=jax.ShapeDtypeStruct((n, D_MODEL), BF16),
        grid=(n // T,),
        in_specs=[pl.BlockSpec(memory_space=pltpu.SMEM), here()] + window(D_MODEL) + window(V_REP_WIDTH)
        + [here(Z_ATT_GATE)],
        out_specs=here(),
        compiler_params=pltpu.CompilerParams(
            dimension_semantics=("arbitrary",), vmem_limit_bytes=VMEM_LIMIT_BYTES),
        name="window_attn",
    )(attn_sink, qp, kp, kp, kp, vp, vp, vp, z)


def _out_kernel(x_ref, uc_ref, o_ref, rg0_ref, rg1_ref, ua_ref, g0_ref, g1_ref, g2_ref,
                wc_ref, wr_ref, wa_ref, wo_ref, out_ref, ur_ref):
    def gate(g_ref):
        return g_ref[...].astype(F32)

    for r in range(OUT_TM // OUT_NORM_ROWS):
        rows = slice(r * OUT_NORM_ROWS, (r + 1) * OUT_NORM_ROWS)
        for h in range(RET_HEADS):
            cols = slice(h * RET_V_DIM, (h + 1) * RET_V_DIM)
            o = o_ref[rows, cols].astype(F32)
            mu = jnp.mean(o, axis=-1, keepdims=True)
            d = o - mu
            var = jnp.mean(d * d, axis=-1, keepdims=True)
            rg_ref = (rg0_ref, rg1_ref)[h * RET_V_DIM // D_MODEL]
            lo = h * RET_V_DIM % D_MODEL
            g = rg_ref[rows, lo:lo + RET_V_DIM].astype(F32)
            ur_ref[rows, cols] = (d * lax.rsqrt(var + EPS) * _silu(g)).astype(BF16)

    merged = gate(g0_ref) * jnp.dot(uc_ref[...], wc_ref[...], preferred_element_type=F32)
    merged = merged + gate(g1_ref) * jnp.dot(ur_ref[...], wr_ref[...], preferred_element_type=F32)
    merged = merged + gate(g2_ref) * jnp.dot(ua_ref[...], wa_ref[...], preferred_element_type=F32)
    out_ref[...] = x_ref[...] + jnp.dot(merged.astype(BF16), wo_ref[...], preferred_element_type=F32)


def _out_proj(x2, u_conv, u_ret, u_attn, z, seq, w_conv_out, w_ret_out, w_attn_out, w_out, layer):
    n = x2.shape[0]
    n_chunks = seq // OUT_TM
    half = n_chunks // 2

    def rows(width, col=0):
        return pl.BlockSpec((OUT_TM, width), lambda i: (i, col // width))

    def whole(shape):
        return pl.BlockSpec((None,) + shape, lambda i: (layer, 0, 0))

    def ret_index(i):
        b, c = i // n_chunks, i % n_chunks
        upper = c >= half
        return (b, jnp.where(upper, c - half, half - 1 - c), jnp.where(upper, 0, 1), 0, 0)

    return pl.pallas_call(
        _out_kernel,
        out_shape=jax.ShapeDtypeStruct((n, D_MODEL), F32),
        grid=(n // OUT_TM,),
        in_specs=[rows(D_MODEL), rows(D_MODEL),
                  pl.BlockSpec((None, None, None, OUT_TM, 2 * D_MODEL), ret_index),
                  rows(D_MODEL, Z_RET_GATE), rows(D_MODEL, Z_RET_GATE + D_MODEL),
                  rows(D_MODEL),
                  rows(D_MODEL, Z_MERGE), rows(D_MODEL, Z_MERGE + D_MODEL), rows(D_MODEL, Z_MERGE + 2 * D_MODEL),
                  whole((D_MODEL, D_MODEL)), whole((2 * D_MODEL, D_MODEL)),
                  whole((D_MODEL, D_MODEL)), whole((D_MODEL, D_MODEL))],
        out_specs=rows(D_MODEL),
        scratch_shapes=[pltpu.VMEM((OUT_TM, RET_HEADS * RET_V_DIM), BF16)],
        compiler_params=pltpu.CompilerParams(
            dimension_semantics=("arbitrary",), vmem_limit_bytes=VMEM_LIMIT_BYTES),
        name="out_proj",
    )(x2, u_conv, u_ret, z, z, u_attn, z, z, z, w_conv_out, w_ret_out, w_attn_out, w_out)


def _rotary_tables(seq):
    pos = jnp.arange(seq, dtype=jnp.int32).astype(F32)
    half_r = RET_QK_DIM // 2
    inv_r = RET_THETA ** (-jnp.arange(half_r, dtype=F32) / half_r)
    ang_r = pos[:, None] * inv_r[None, :]
    half_a = ROPE_DIM // 2
    inv_a = ROPE_THETA ** (-jnp.arange(half_a, dtype=F32) / half_a)
    ang_a = pos[:, None] * inv_a[None, :]
    ca, sa = jnp.cos(ang_a), jnp.sin(ang_a)
    rest = HEAD_DIM - ROPE_DIM
    cos64 = jnp.concatenate([ca, ca, jnp.ones((seq, rest), F32)], axis=1)
    sin64 = jnp.concatenate([-sa, sa, jnp.zeros((seq, rest), F32)], axis=1)
    rep = LANES // HEAD_DIM
    return jnp.cos(ang_r), jnp.sin(ang_r), jnp.tile(cos64, (1, rep)), jnp.tile(sin64, (1, rep))


def kernel(x, norm_g, w_in, b_gate, conv_dw, conv_b, conv_ln_g, conv_ln_b, ret_decay,
           q_norm_g, k_norm_g, attn_sink, w_conv_out, w_ret_out, w_attn_out, w_out):
    batch, seq, d = x.shape
    depth = norm_g.shape[0]
    assert d == D_MODEL and seq % max(CONV_TILE, 2 * RET_CHUNK, PREP_TILE, ATTN_BLOCK) == 0
    assert (batch * seq) % PROJ_TM == 0 and w_in.shape[-1] == Z_WIDTH
    cos_r, sin_r, cos_a, sin_a = _rotary_tables(seq)
    x2 = x.reshape(batch * seq, d)
    out_weights = [w.astype(BF16) for w in (w_conv_out, w_ret_out, w_attn_out, w_out)]
    for l in range(depth):
        z = _in_proj(x2, norm_g, w_in, b_gate, cos_r, sin_r, l)
        qp, kp, vp = _prep(z, seq, cos_a, sin_a, q_norm_g[l], k_norm_g[l])
        u_conv = _conv_branch(z, seq, conv_dw[l], conv_b[l], conv_ln_g[l], conv_ln_b[l])
        u_ret = _ret_branch(z, batch, seq, ret_decay[l])
        u_attn = _attn_branch(z, qp, kp, vp, seq, attn_sink[l])
        x2 = _out_proj(x2, u_conv, u_ret, u_attn, z, seq, *out_weights, l)
    return x2.reshape(batch, seq, d)
```

```python
import functools
import math

import jax
import jax.numpy as jnp
from jax import lax
from jax.experimental import pallas as pl
from jax.experimental.pallas import tpu as pltpu

F32 = jnp.float32
BF16 = jnp.bfloat16

D_MODEL = 1024
HEAD_DIM = 64
EPS = 1e-6
CONV_KERNEL = 31
CONV_HALF = CONV_KERNEL // 2
RET_HEADS = 4
RET_QK_DIM = 256
RET_V_DIM = 512
RET_THETA = 10000.0
ATTN_HEADS = 16
ATTN_KV_HEADS = 4
ATTN_GROUP = ATTN_HEADS // ATTN_KV_HEADS
ATTN_BLOCK = 128
ROPE_THETA = 500000.0
ROPE_DIM = HEAD_DIM // 4
KV_WIDTH = ATTN_KV_HEADS * HEAD_DIM
LOG2_E = math.log2(math.e)

LANES = 128
VMEM_LIMIT_BYTES = 56 * 1024 * 1024

Z_CONV_A, Z_CONV_B, Z_CONV_GATE = 0, 1024, 2048
Z_RET_Q, Z_RET_K, Z_RET_V, Z_RET_GATE = 3072, 4096, 5120, 7168
Z_ATT_Q, Z_ATT_GATE, Z_MERGE, Z_ATT_KV = 9216, 10240, 11264, 14336
Z_WIDTH = Z_ATT_KV + 2 * KV_WIDTH
REF_ATT_KV = Z_ATT_Q + D_MODEL

PROJ_TM, PROJ_TN, PROJ_ROT_ROWS = 2048, 512, 512
NORM_ROWS = 256
CONV_TILE, CONV_ROWS, CONV_HALO, CONV_TOKENS, CONV_NORM_UNROLL = 512, 32, 16, 32, 4
CBLK = D_MODEL // LANES
RET_CHUNK = 256
RET_HPS = 2
PREP_TILE, PREP_ROWS = 512, 128
V_REP_WIDTH = ATTN_KV_HEADS * LANES
OUT_TM = RET_CHUNK
OUT_NORM_ROWS = 64


def _sigmoid(x):
    return 1.0 / (1.0 + jnp.exp(-x))


def _silu(x):
    return x * _sigmoid(x)


def _in_proj_kernel(x_ref, g_ref, w_ref, cos_ref, sin_ref, bg_ref, z_ref, h_ref):
    j = pl.program_id(1)

    @pl.when(j == 0)
    def _():
        def norm_rows(r, carry):
            rows = pl.ds(pl.multiple_of(r * NORM_ROWS, NORM_ROWS), NORM_ROWS)
            xv = x_ref[rows, :]
            ms = jnp.mean(xv * xv, axis=-1, keepdims=True)
            h_ref[rows, :] = (xv * lax.rsqrt(ms + EPS) * g_ref[...]).astype(BF16)
            return carry

        lax.fori_loop(0, PROJ_TM // NORM_ROWS, norm_rows, 0)

    def product():
        return jnp.dot(h_ref[...], w_ref[...].astype(BF16), preferred_element_type=F32)

    rotary_tile = (j >= Z_RET_Q // PROJ_TN) & (j < Z_RET_V // PROJ_TN)
    merge_tile = (j >= Z_MERGE // PROJ_TN) & (j < Z_ATT_KV // PROJ_TN)

    @pl.when(jnp.logical_not(rotary_tile | merge_tile))
    def _():
        z_ref[...] = product().astype(BF16)

    @pl.when(merge_tile)
    def _():
        w = w_ref[...].astype(BF16)
        for r in range(PROJ_TM // PROJ_ROT_ROWS):
            rows = slice(r * PROJ_ROT_ROWS, (r + 1) * PROJ_ROT_ROWS)
            acc = jnp.dot(h_ref[rows, :], w, preferred_element_type=F32)
            z_ref[rows, :] = _sigmoid(acc + bg_ref[...]).astype(BF16)

    @pl.when(rotary_tile)
    def _():
        scale = jnp.where(j >= Z_RET_K // PROJ_TN, RET_QK_DIM ** -0.5, 1.0)
        half = RET_QK_DIM // 2
        w = w_ref[...].astype(BF16)
        for r in range(PROJ_TM // PROJ_ROT_ROWS):
            rows = slice(r * PROJ_ROT_ROWS, (r + 1) * PROJ_ROT_ROWS)
            acc = jnp.dot(h_ref[rows, :], w, preferred_element_type=F32)
            c = cos_ref[rows, :] * scale
            s = sin_ref[rows, :] * scale
            for hd in range(PROJ_TN // RET_QK_DIM):
                lo = slice(hd * RET_QK_DIM, hd * RET_QK_DIM + half)
                hi = slice(hd * RET_QK_DIM + half, (hd + 1) * RET_QK_DIM)
                z_ref[rows, lo] = (acc[:, lo] * c - acc[:, hi] * s).astype(BF16)
                z_ref[rows, hi] = (acc[:, hi] * c + acc[:, lo] * s).astype(BF16)


def _in_proj(x2, norm_g, w_in, b_gate, cos_r, sin_r, layer):
    n = x2.shape[0]
    depth = norm_g.shape[0]
    seq_tiles = cos_r.shape[0] // PROJ_TM
    table = pl.BlockSpec((PROJ_TM, LANES), lambda i, j: (i % seq_tiles, 0))
    merge_tiles = 3 * D_MODEL // PROJ_TN
    bias = pl.BlockSpec((None, 1, PROJ_TN),
                        lambda i, j: (layer, 0, jnp.clip(j - Z_MERGE // PROJ_TN, 0, merge_tiles - 1)))
    kv_tile = REF_ATT_KV // PROJ_TN
    n_tiles = Z_WIDTH // PROJ_TN
    assert REF_ATT_KV % PROJ_TN == 0 and 2 * KV_WIDTH == PROJ_TN

    def w_tile(j):
        return jnp.where(j == n_tiles - 1, kv_tile, jnp.where(j >= kv_tile, j + 1, j))

    return pl.pallas_call(
        _in_proj_kernel,
        out_shape=jax.ShapeDtypeStruct((n, Z_WIDTH), BF16),
        grid=(n // PROJ_TM, n_tiles),
        in_specs=[
            pl.BlockSpec((PROJ_TM, D_MODEL), lambda i, j: (i, 0)),
            pl.BlockSpec((None, 1, D_MODEL), lambda i, j: (layer, 0, 0)),
            pl.BlockSpec((None, D_MODEL, PROJ_TN), lambda i, j: (layer, 0, w_tile(j))),
            table, table, bias,
        ],
        out_specs=pl.BlockSpec((PROJ_TM, PROJ_TN), lambda i, j: (i, j)),
        scratch_shapes=[pltpu.VMEM((PROJ_TM, D_MODEL), BF16)],
        compiler_params=pltpu.CompilerParams(
            dimension_semantics=("arbitrary", "arbitrary"), vmem_limit_bytes=VMEM_LIMIT_BYTES),
        name="in_proj",
    )(x2, norm_g.reshape(depth, 1, D_MODEL), w_in, cos_r, sin_r, b_gate.reshape(depth, 1, 3 * D_MODEL))


def _prep_kernel(aq_ref, akv_ref, cosa_ref, sina_ref, gq_ref, gk_ref, avg_ref, swap_ref,
                 repk_ref, repv_ref, qp_ref, kp_ref, vp_ref, k_ref):
    avg = avg_ref[...]
    swap = swap_ref[...]
    q_scale = LOG2_E * HEAD_DIM ** -0.5

    def chunk(r, carry):
        rows = pl.ds(pl.multiple_of(r * PREP_ROWS, PREP_ROWS), PREP_ROWS)
        cos_a = cosa_ref[rows, :]
        sin_a = sina_ref[rows, :]

        def norm_rope(x, g_ref, scale):
            ms = jnp.dot((x * x).astype(BF16), avg, preferred_element_type=F32)
            xg = x * g_ref[...]
            partner = jnp.dot(xg.astype(BF16), swap, preferred_element_type=F32)
            return (xg * cos_a + partner * sin_a) * (lax.rsqrt(ms + EPS) * scale)

        for c in range(D_MODEL // LANES):
            lanes = slice(c * LANES, (c + 1) * LANES)
            qp_ref[rows, lanes] = norm_rope(aq_ref[rows, lanes].astype(F32), gq_ref, q_scale).astype(BF16)
        for c in range(KV_WIDTH // LANES):
            lanes = slice(c * LANES, (c + 1) * LANES)
            k_ref[rows, lanes] = norm_rope(akv_ref[rows, lanes].astype(F32), gk_ref, 1.0).astype(BF16)
        return carry

    lax.fori_loop(0, PREP_TILE // PREP_ROWS, chunk, 0)
    kp_ref[...] = jnp.dot(k_ref[...], repk_ref[...], preferred_element_type=F32).astype(BF16)
    vp_ref[...] = jnp.dot(akv_ref[:, KV_WIDTH:], repv_ref[...], preferred_element_type=F32).astype(BF16)


def _prep(z, seq, cos_a, sin_a, q_norm_g, k_norm_g):
    n = z.shape[0]
    seq_tiles = seq // PREP_TILE
    heads_per_block = LANES // HEAD_DIM
    gq = jnp.tile(q_norm_g, heads_per_block).reshape(1, LANES)
    gk = jnp.tile(k_norm_g, heads_per_block).reshape(1, LANES)
    lane = jnp.arange(LANES)
    head = lane // HEAD_DIM
    avg = jnp.where(head[:, None] == head[None, :], 1.0 / HEAD_DIM, 0.0).astype(BF16)
    within = lane % HEAD_DIM
    half = ROPE_DIM // 2
    partner = jnp.where(within < half, lane + half, jnp.where(within < ROPE_DIM, lane - half, -1))
    swap = (lane[:, None] == partner[None, :]).astype(BF16)
    src = jnp.arange(KV_WIDTH)

    def rep(slots):
        dst = jnp.arange(ATTN_KV_HEADS * slots * HEAD_DIM)
        return ((src[:, None] // HEAD_DIM == dst[None, :] // (slots * HEAD_DIM))
                & (src[:, None] % HEAD_DIM == dst[None, :] % HEAD_DIM)).astype(BF16)

    def rows(width, col):
        return pl.BlockSpec((PREP_TILE, width), lambda i: (i, col // width))

    def whole(shape):
        return pl.BlockSpec(shape, lambda i: (0, 0))

    tab = pl.BlockSpec((PREP_TILE, LANES), lambda i: (i % seq_tiles, 0))
    wide = jax.ShapeDtypeStruct((n, D_MODEL), BF16)
    wide_spec = pl.BlockSpec((PREP_TILE, D_MODEL), lambda i: (i, 0))
    return pl.pallas_call(
        _prep_kernel,
        out_shape=(wide, wide, jax.ShapeDtypeStruct((n, V_REP_WIDTH), BF16)),
        grid=(n // PREP_TILE,),
        in_specs=[rows(D_MODEL, Z_ATT_Q), rows(2 * KV_WIDTH, Z_ATT_KV),
                  tab, tab, whole((1, LANES)), whole((1, LANES)),
                  whole((LANES, LANES)), whole((LANES, LANES)),
                  whole((KV_WIDTH, D_MODEL)), whole((KV_WIDTH, V_REP_WIDTH))],
        out_specs=(wide_spec, wide_spec, pl.BlockSpec((PREP_TILE, V_REP_WIDTH), lambda i: (i, 0))),
        scratch_shapes=[pltpu.VMEM((PREP_TILE, KV_WIDTH), BF16)],
        compiler_params=pltpu.CompilerParams(
            dimension_semantics=("arbitrary",), vmem_limit_bytes=VMEM_LIMIT_BYTES),
        name="qk_prep",
    )(z, z, cos_a, sin_a, gq, gk, avg, swap, rep(ATTN_GROUP), rep(LANES // HEAD_DIM))


def _conv_kernel(seq_tiles, a_ref, b_ref, gate_ref, ap_ref, bp_ref, an_ref, bn_ref,
                 w_ref, cb_ref, lg_ref, lb_ref, out_ref, v_ref, y_ref):
    t = pl.program_id(0) % seq_tiles

    def glu(a, b):
        return a.astype(F32) * _sigmoid(b.astype(F32))

    def scatter(val, tok0):
        for c in range(CBLK):
            v_ref[pl.ds(tok0 * CBLK + c, val.shape[0], stride=CBLK), :] = val[:, c * LANES:(c + 1) * LANES]

    scatter(jnp.where(t == 0, 0.0, glu(ap_ref[...], bp_ref[...])), 0)
    scatter(jnp.where(t == seq_tiles - 1, 0.0, glu(an_ref[...], bn_ref[...])), CONV_HALO + CONV_TILE)

    def fill(r, carry):
        for u in range(CONV_NORM_UNROLL):
            r0 = pl.multiple_of((r * CONV_NORM_UNROLL + u) * CONV_ROWS, CONV_ROWS)
            rows = pl.ds(r0, CONV_ROWS)
            scatter(glu(a_ref[rows, :], b_ref[rows, :]), r0 + CONV_HALO)
        return carry

    lax.fori_loop(0, CONV_TILE // (CONV_ROWS * CONV_NORM_UNROLL), fill, 0)

    bias = cb_ref[...]

    def conv_tokens(r, carry):
        t0 = r * CONV_TOKENS
        acc = [bias] * CONV_TOKENS
        for k in range(CONV_KERNEL):
            wk = w_ref[k * CBLK:(k + 1) * CBLK, :]
            for i in range(CONV_TOKENS):
                src = pl.multiple_of((t0 + (i + k - CONV_HALF + CONV_HALO)) * CBLK, CBLK)
                acc[i] = acc[i] + v_ref[pl.ds(src, CBLK), :] * wk
        for i in range(CONV_TOKENS):
            y_ref[pl.ds(pl.multiple_of((t0 + i) * CBLK, CBLK), CBLK), :] = acc[i]
        return carry

    lax.fori_loop(0, CONV_TILE // CONV_TOKENS, conv_tokens, 0)

    def norm_rows(r, carry):
        for u in range(CONV_NORM_UNROLL):
            r0 = pl.multiple_of((r * CONV_NORM_UNROLL + u) * CONV_ROWS, CONV_ROWS)
            y = jnp.concatenate(
                [y_ref[pl.ds(r0 * CBLK + c, CONV_ROWS, stride=CBLK), :] for c in range(CBLK)], axis=1)
            mu = jnp.mean(y, axis=-1, keepdims=True)
            d = y - mu
            var = jnp.mean(d * d, axis=-1, keepdims=True)
            yn = d * lax.rsqrt(var + EPS) * lg_ref[...] + lb_ref[...]
            g = gate_ref[pl.ds(r0, CONV_ROWS), :].astype(F32)
            out_ref[pl.ds(r0, CONV_ROWS), :] = (_silu(yn) * _silu(g)).astype(BF16)
        return carry

    lax.fori_loop(0, CONV_TILE // (CONV_ROWS * CONV_NORM_UNROLL), norm_rows, 0)


def _conv_branch(z, seq, conv_dw, conv_b, ln_g, ln_b):
    n = z.shape[0]
    seq_tiles = seq // CONV_TILE
    halo_per_tile = CONV_TILE // CONV_HALO
    last_halo = n // CONV_HALO - 1
    w_tiles = conv_dw.reshape(CONV_KERNEL * CBLK, LANES)

    def main(col):
        return pl.BlockSpec((CONV_TILE, D_MODEL), lambda i: (i, col // D_MODEL))

    def prev(col):
        return pl.BlockSpec((CONV_HALO, D_MODEL),
                            lambda i: (jnp.maximum(i * halo_per_tile - 1, 0), col // D_MODEL))

    def nxt(col):
        return pl.BlockSpec((CONV_HALO, D_MODEL),
                            lambda i: (jnp.minimum((i + 1) * halo_per_tile, last_halo), col // D_MODEL))

    vec = pl.BlockSpec((1, D_MODEL), lambda i: (0, 0))
    return pl.pallas_call(
        functools.partial(_conv_kernel, seq_tiles),
        out_shape=jax.ShapeDtypeStruct((n, D_MODEL), BF16),
        grid=(n // CONV_TILE,),
        in_specs=[main(Z_CONV_A), main(Z_CONV_B), main(Z_CONV_GATE),
                  prev(Z_CONV_A), prev(Z_CONV_B), nxt(Z_CONV_A), nxt(Z_CONV_B),
                  pl.BlockSpec((CONV_KERNEL * CBLK, LANES), lambda i: (0, 0)),
                  pl.BlockSpec((CBLK, LANES), lambda i: (0, 0)), vec, vec],
        out_specs=pl.BlockSpec((CONV_TILE, D_MODEL), lambda i: (i, 0)),
        scratch_shapes=[pltpu.VMEM(((CONV_TILE + 2 * CONV_HALO) * CBLK, LANES), F32),
                        pltpu.VMEM((CONV_TILE * CBLK, LANES), F32)],
        compiler_params=pltpu.CompilerParams(
            dimension_semantics=("arbitrary",), vmem_limit_bytes=VMEM_LIMIT_BYTES),
        name="conv_branch",
    )(z, z, z, z, z, z, z, w_tiles, conv_b.reshape(CBLK, LANES), ln_g.reshape(1, D_MODEL),
      ln_b.reshape(1, D_MODEL))


def _ret_kernel(n_chunks, rd_ref, qf_ref, kf_ref, vf_ref, qb_ref, kb_ref, vb_ref,
                out_ref, oacc_ref, state_ref, dmat_ref, dec_ref, cdec_ref):
    L = RET_CHUNK
    head0 = (pl.program_id(0) % (RET_HEADS // RET_HPS)) * RET_HPS
    c = pl.program_id(1)

    @pl.when(c == 0)
    def _():
        idx = lax.broadcasted_iota(jnp.int32, (L, LANES), 0).astype(F32)
        i = lax.broadcasted_iota(jnp.int32, (L, L), 0)
        j = lax.broadcasted_iota(jnp.int32, (L, L), 1)
        d = (i - j).astype(F32)
        state_ref[...] = jnp.zeros_like(state_ref)
        for hd in range(RET_HPS):
            lg_f = -jnp.exp(jnp.full((L, LANES), rd_ref[0, head0 + hd], F32))
            lg_b = -jnp.exp(jnp.full((L, LANES), rd_ref[1, head0 + hd], F32))
            dec_ref[hd, 0] = jnp.exp(lg_f * (idx + 1.0))
            dec_ref[hd, 1] = jnp.exp(lg_f * (L - 1.0 - idx))
            dec_ref[hd, 2] = jnp.exp(lg_b * (L - idx))
            dec_ref[hd, 3] = jnp.exp(lg_b * idx)
            cdec_ref[hd, 0] = jnp.exp(lg_f[:RET_QK_DIM] * float(L))
            cdec_ref[hd, 1] = jnp.exp(lg_b[:RET_QK_DIM] * float(L))
            lgm_f = -jnp.exp(jnp.full((L, L), rd_ref[0, head0 + hd], F32))
            lgm_b = -jnp.exp(jnp.full((L, L), rd_ref[1, head0 + hd], F32))
            dmat_ref[hd] = jnp.where(i >= j, jnp.exp(lgm_f * jnp.maximum(d, 0.0)),
                                     jnp.exp(lgm_b * jnp.maximum(-d, 0.0)))

    rows_f = pl.ds(pl.multiple_of(c * L, L), L)
    rows_b = pl.ds(pl.multiple_of((n_chunks - 1 - c) * L, L), L)
    parts = []
    for hd in range(RET_HPS):
        qk_cols = slice(hd * RET_QK_DIM, (hd + 1) * RET_QK_DIM)
        v_cols = slice(hd * RET_V_DIM, (hd + 1) * RET_V_DIM)

        def decayed(ref, dec):
            return (ref[:, qk_cols].astype(F32)
                    * jnp.concatenate([dec] * (RET_QK_DIM // LANES), axis=1)).astype(BF16)

        def advance(direction, q_ref, k_ref, v_ref):
            state = state_ref[hd, direction]
            cross = jnp.dot(decayed(q_ref, dec_ref[hd, 2 * direction]), state.astype(BF16),
                            preferred_element_type=F32)
            kv = lax.dot_general(decayed(k_ref, dec_ref[hd, 2 * direction + 1]), v_ref[:, v_cols],
                                 (((0,), (0,)), ((), ())), preferred_element_type=F32)
            chunk_dec = jnp.concatenate([cdec_ref[hd, direction]] * (RET_V_DIM // LANES), axis=1)
            state_ref[hd, direction] = chunk_dec * state + kv
            return cross

        s = lax.dot_general(qf_ref[:, qk_cols], kf_ref[:, qk_cols], (((1,), (1,)), ((), ())),
                            preferred_element_type=F32)
        part_f = jnp.dot((s * dmat_ref[hd]).astype(BF16), vf_ref[:, v_cols], preferred_element_type=F32)
        part_f = part_f + advance(0, qf_ref, kf_ref, vf_ref)
        part_b = advance(1, qb_ref, kb_ref, vb_ref)
        parts.append((v_cols, part_f, part_b))

    @pl.when(c < n_chunks // 2)
    def _():
        for v_cols, part_f, part_b in parts:
            oacc_ref[rows_f, v_cols] = part_f
            oacc_ref[rows_b, v_cols] = part_b

    @pl.when(c >= n_chunks // 2)
    def _():
        for v_cols, part_f, part_b in parts:
            out_ref[0, :, v_cols] = (oacc_ref[rows_f, v_cols] + part_f).astype(BF16)
            out_ref[1, :, v_cols] = (oacc_ref[rows_b, v_cols] + part_b).astype(BF16)


def _ret_branch(z, batch, seq, ret_decay):
    L = RET_CHUNK
    n_chunks = seq // L
    half = n_chunks // 2
    groups = RET_HEADS // RET_HPS
    assert n_chunks % 2 == 0 and RET_HEADS % RET_HPS == 0

    def fwd(bg, c):
        return (bg // groups) * n_chunks + c

    def bwd(bg, c):
        return (bg // groups) * n_chunks + (n_chunks - 1 - c)

    def zcol(col, width, bg):
        return col // width + bg % groups

    qk = (L, RET_HPS * RET_QK_DIM)
    vv = (L, RET_HPS * RET_V_DIM)
    in_specs = [pl.BlockSpec(memory_space=pltpu.SMEM)]
    for row in (fwd, bwd):
        in_specs += [
            pl.BlockSpec(qk, lambda bg, c, row=row: (row(bg, c), zcol(Z_RET_Q, qk[1], bg))),
            pl.BlockSpec(qk, lambda bg, c, row=row: (row(bg, c), zcol(Z_RET_K, qk[1], bg))),
            pl.BlockSpec(vv, lambda bg, c, row=row: (row(bg, c), zcol(Z_RET_V, vv[1], bg))),
        ]
    return pl.pallas_call(
        functools.partial(_ret_kernel, n_chunks),
        out_shape=jax.ShapeDtypeStruct((batch, half, 2, L, RET_HEADS * RET_V_DIM), BF16),
        grid=(batch * groups, n_chunks),
        in_specs=in_specs,
        out_specs=pl.BlockSpec((None, None, 2, L, vv[1]),
                               lambda bg, c: (bg // groups, jnp.maximum(c - half, 0), 0, 0, bg % groups)),
        scratch_shapes=[
            pltpu.VMEM((seq, vv[1]), F32),
            pltpu.VMEM((RET_HPS, 2, RET_QK_DIM, RET_V_DIM), F32),
            pltpu.VMEM((RET_HPS, L, L), F32),
            pltpu.VMEM((RET_HPS, 4, L, LANES), F32),
            pltpu.VMEM((RET_HPS, 2, RET_QK_DIM, LANES), F32),
        ],
        compiler_params=pltpu.CompilerParams(
            dimension_semantics=("arbitrary", "arbitrary"), vmem_limit_bytes=VMEM_LIMIT_BYTES),
        name="retention",
    )(ret_decay, z, z, z, z, z, z)


def _attn_kernel(seq_blocks, sink_ref, q_ref, kl_ref, km_ref, kr_ref, vl_ref, vm_ref, vr_ref,
                 gate_ref, out_ref):
    T = ATTN_BLOCK
    GW = ATTN_GROUP * HEAD_DIM
    blk = pl.program_id(0) % seq_blocks
    lane = lax.broadcasted_iota(jnp.int32, (T, GW), 1)
    head_masks = [lane // HEAD_DIM == h for h in range(ATTN_GROUP)]
    row = lax.broadcasted_iota(jnp.int32, (T, T), 0)
    col = lax.broadcasted_iota(jnp.int32, (T, T), 1)
    mask_l = (col >= row) & (blk > 0)
    mask_r = (col <= row) & (blk < seq_blocks - 1)
    lane_o = lax.broadcasted_iota(jnp.int32, (T, LANES), 1)

    def scores(g):
        lanes = slice(g * GW, (g + 1) * GW)
        qg = q_ref[:, lanes]
        q_heads = jnp.concatenate([jnp.where(m, qg, jnp.zeros_like(qg)) for m in head_masks], axis=0)
        k_win = jnp.concatenate([kl_ref[:, lanes], km_ref[:, lanes], kr_ref[:, lanes]], axis=0)
        return lax.dot_general(q_heads, k_win, (((1,), (1,)), ((), ())), preferred_element_type=F32)

    def softmax(g, s):
        ps, invs = [], []
        for h in range(ATTN_GROUP):
            rows = slice(h * T, (h + 1) * T)
            sl = jnp.where(mask_l, s[rows, :T], -1e30)
            sm = s[rows, T:2 * T]
            sr = jnp.where(mask_r, s[rows, 2 * T:], -1e30)
            sink = sink_ref[g * ATTN_GROUP + h] * LOG2_E
            m = jnp.maximum(jnp.max(jnp.maximum(jnp.maximum(sl, sm), sr), axis=-1, keepdims=True), sink)
            el = jnp.exp2(sl - m)
            em = jnp.exp2(sm - m)
            er = jnp.exp2(sr - m)
            invs.append(1.0 / (jnp.sum(el + em + er, axis=-1, keepdims=True) + jnp.exp2(sink - m)))
            ps.append(jnp.concatenate([el, em, er], axis=1).astype(BF16))
        return ps, invs

    def values(g, ps, invs):
        lanes = slice(g * GW, (g + 1) * GW)
        outs = []
        v_win = jnp.concatenate([v_ref[:, g * LANES:(g + 1) * LANES] for v_ref in (vl_ref, vm_ref, vr_ref)],
                                axis=0)
        for hb in range(GW // LANES):
            o2 = jnp.dot(jnp.concatenate(ps[2 * hb:2 * hb + 2], axis=0), v_win, preferred_element_type=F32)
            outs.append(jnp.where(lane_o < HEAD_DIM, o2[:T] * invs[2 * hb], o2[T:] * invs[2 * hb + 1]))
        gate = gate_ref[:, lanes].astype(F32)
        out_ref[:, lanes] = (jnp.concatenate(outs, axis=1) * _silu(gate)).astype(BF16)

    ss = [scores(g) for g in range(ATTN_KV_HEADS)]
    ps = [softmax(g, ss[g]) for g in range(ATTN_KV_HEADS)]
    for g in range(ATTN_KV_HEADS):
        values(g, *ps[g])


def _attn_branch(z, qp, kp, vp, seq, attn_sink):
    n = z.shape[0]
    T = ATTN_BLOCK
    seq_blocks = seq // T
    last = n // T - 1
    blk = (T, D_MODEL)

    def here(col=0):
        return pl.BlockSpec(blk, lambda i: (i, col // D_MODEL))

    def window(width):
        return [pl.BlockSpec((T, width), lambda i: (jnp.maximum(i - 1, 0), 0)),
                pl.BlockSpec((T, width), lambda i: (i, 0)),
                pl.BlockSpec((T, width), lambda i: (jnp.minimum(i + 1, last), 0))]

    return pl.pallas_call(
        functools.partial(_attn_kernel, seq_blocks),
        out_shape=jax.ShapeDtypeStruct((n, D_MODEL), BF16),
        grid=(n // T,),
        in_specs=[pl.BlockSpec(memory_space=pltpu.SMEM), here()] + window(D_MODEL) + window(V_REP_WIDTH)
        + [here(Z_ATT_GATE)],
        out_specs=here(),
        compiler_params=pltpu.CompilerParams(
            dimension_semantics=("arbitrary",), vmem_limit_bytes=VMEM_LIMIT_BYTES),
        name="window_attn",
    )(attn_sink, qp, kp, kp, kp, vp, vp, vp, z)


def _out_kernel(x_ref, uc_ref, o_ref, rg0_ref, rg1_ref, ua_ref, g0_ref, g1_ref, g2_ref,
                wc_ref, wr_ref, wa_ref, wo_ref, out_ref, ur_ref):
    def gate(g_ref):
        return g_ref[...].astype(F32)

    for r in range(OUT_TM // OUT_NORM_ROWS):
        rows = slice(r * OUT_NORM_ROWS, (r + 1) * OUT_NORM_ROWS)
        for h in range(RET_HEADS):
            cols = slice(h * RET_V_DIM, (h + 1) * RET_V_DIM)
            o = o_ref[rows, cols].astype(F32)
            mu = jnp.mean(o, axis=-1, keepdims=True)
            d = o - mu
            var = jnp.mean(d * d, axis=-1, keepdims=True)
            rg_ref = (rg0_ref, rg1_ref)[h * RET_V_DIM // D_MODEL]
            lo = h * RET_V_DIM % D_MODEL
            g = rg_ref[rows, lo:lo + RET_V_DIM].astype(F32)
            ur_ref[rows, cols] = (d * lax.rsqrt(var + EPS) * _silu(g)).astype(BF16)

    merged = gate(g0_ref) * jnp.dot(uc_ref[...], wc_ref[...], preferred_element_type=F32)
    merged = merged + gate(g1_ref) * jnp.dot(ur_ref[...], wr_ref[...], preferred_element_type=F32)
    merged = merged + gate(g2_ref) * jnp.dot(ua_ref[...], wa_ref[...], preferred_element_type=F32)
    out_ref[...] = x_ref[...] + jnp.dot(merged.astype(BF16), wo_ref[...], preferred_element_type=F32)


def _out_proj(x2, u_conv, u_ret, u_attn, z, seq, w_conv_out, w_ret_out, w_attn_out, w_out, layer):
    n = x2.shape[0]
    n_chunks = seq // OUT_TM
    half = n_chunks // 2

    def rows(width, col=0):
        return pl.BlockSpec((OUT_TM, width), lambda i: (i, col // width))

    def whole(shape):
        return pl.BlockSpec((None,) + shape, lambda i: (layer, 0, 0))

    def ret_index(i):
        b, c = i // n_chunks, i % n_chunks
        upper = c >= half
        return (b, jnp.where(upper, c - half, half - 1 - c), jnp.where(upper, 0, 1), 0, 0)

    return pl.pallas_call(
        _out_kernel,
        out_shape=jax.ShapeDtypeStruct((n, D_MODEL), F32),
        grid=(n // OUT_TM,),
        in_specs=[rows(D_MODEL), rows(D_MODEL),
                  pl.BlockSpec((None, None, None, OUT_TM, 2 * D_MODEL), ret_index),
                  rows(D_MODEL, Z_RET_GATE), rows(D_MODEL, Z_RET_GATE + D_MODEL),
                  rows(D_MODEL),
                  rows(D_MODEL, Z_MERGE), rows(D_MODEL, Z_MERGE + D_MODEL), rows(D_MODEL, Z_MERGE + 2 * D_MODEL),
                  whole((D_MODEL, D_MODEL)), whole((2 * D_MODEL, D_MODEL)),
                  whole((D_MODEL, D_MODEL)), whole((D_MODEL, D_MODEL))],
        out_specs=rows(D_MODEL),
        scratch_shapes=[pltpu.VMEM((OUT_TM, RET_HEADS * RET_V_DIM), BF16)],
        compiler_params=pltpu.CompilerParams(
            dimension_semantics=("arbitrary",), vmem_limit_bytes=VMEM_LIMIT_BYTES),
        name="out_proj",
    )(x2, u_conv, u_ret, z, z, u_attn, z, z, z, w_conv_out, w_ret_out, w_attn_out, w_out)


def _rotary_tables(seq):
    pos = jnp.arange(seq, dtype=jnp.int32).astype(F32)
    half_r = RET_QK_DIM // 2
    inv_r = RET_THETA ** (-jnp.arange(half_r, dtype=F32) / half_r)
    ang_r = pos[:, None] * inv_r[None, :]
    half_a = ROPE_DIM // 2
    inv_a = ROPE_THETA ** (-jnp.arange(half_a, dtype=F32) / half_a)
    ang_a = pos[:, None] * inv_a[None, :]
    ca, sa = jnp.cos(ang_a), jnp.sin(ang_a)
    rest = HEAD_DIM - ROPE_DIM
    cos64 = jnp.concatenate([ca, ca, jnp.ones((seq, rest), F32)], axis=1)
    sin64 = jnp.concatenate([-sa, sa, jnp.zeros((seq, rest), F32)], axis=1)
    rep = LANES // HEAD_DIM
    return jnp.cos(ang_r), jnp.sin(ang_r), jnp.tile(cos64, (1, rep)), jnp.tile(sin64, (1, rep))


def kernel(x, norm_g, w_in, b_gate, conv_dw, conv_b, conv_ln_g, conv_ln_b, ret_decay,
           q_norm_g, k_norm_g, attn_sink, w_conv_out, w_ret_out, w_attn_out, w_out):
    batch, seq, d = x.shape
    depth = norm_g.shape[0]
    assert d == D_MODEL and seq % max(CONV_TILE, 2 * RET_CHUNK, PREP_TILE, ATTN_BLOCK) == 0
    assert (batch * seq) % PROJ_TM == 0 and w_in.shape[-1] == Z_WIDTH
    cos_r, sin_r, cos_a, sin_a = _rotary_tables(seq)
    x2 = x.reshape(batch * seq, d)
    out_weights = [w.astype(BF16) for w in (w_conv_out, w_ret_out, w_attn_out, w_out)]
    for l in range(depth):
        z = _in_proj(x2, norm_g, w_in, b_gate, cos_r, sin_r, l)
        qp, kp, vp = _prep(z, seq, cos_a, sin_a, q_norm_g[l], k_norm_g[l])
        u_conv = _conv_branch(z, seq, conv_dw[l], conv_b[l], conv_ln_g[l], conv_ln_b[l])
        u_ret = _ret_branch(z, batch, seq, ret_decay[l])
        u_attn = _attn_branch(z, qp, kp, vp, seq, attn_sink[l])
        x2 = _out_proj(x2, u_conv, u_ret, u_attn, z, seq, *out_weights, l)
    return x2.reshape(batch, seq, d)
```
